```python
import jax, jax.numpy as jnp
from jax import lax
import numpy as np

D_MODEL = 1024
BATCH = 4
SEQ = 8192
DEPTH = 2
DEC_BATCH = 32
DEC_SEQ = 1
PAST_LEN = 16384
PAGE_SIZE = 128

N_A_LAYERS = DEPTH // 2
N_B_LAYERS = DEPTH - N_A_LAYERS
POOL_WINDOWS = (2, 4, 8, 16)
N_POOL_GROUPS = len(POOL_WINDOWS)
POOL_GROUP = D_MODEL // N_POOL_GROUPS
POOL_STATE = max(POOL_WINDOWS) - 1
HEAD_DIM = 64
N_HEADS = D_MODEL // HEAD_DIM
D_FF = -(-8 * D_MODEL // (3 * 256)) * 256
Q_BLOCK = 128
EPS = 1e-6
ATTN_SCALE = HEAD_DIM ** -0.5

kernel_name = "yoco_pool_forgetting_attention_step"


def rmsnorm(x, g):
    x32 = x.astype(jnp.float32)
    y = x32 * lax.rsqrt(jnp.mean(x32 * x32, axis=-1, keepdims=True) + EPS)
    return (y * g.astype(jnp.float32)).astype(x.dtype)


def swiglu(h, w_gu, w_d):
    gu = h @ w_gu
    g, u = gu[..., :D_FF], gu[..., D_FF:]
    return (jax.nn.silu(g) * u) @ w_d


def pool_mix(u_ext, n_hist, w_groups, bias, scale):
    L = u_ext.shape[1]
    cs = jnp.cumsum(u_ext.astype(jnp.float32), axis=1)
    cs = jnp.pad(cs, ((0, 0), (1, 0), (0, 0)))
    hi = jnp.arange(n_hist, L) + 1
    u_t = u_ext[:, n_hist:].astype(jnp.float32)
    outs = []
    for g, w in enumerate(POOL_WINDOWS):
        sl = slice(g * POOL_GROUP, (g + 1) * POOL_GROUP)
        lo = jnp.maximum(hi - w, 0)
        cnt = (hi - lo).astype(jnp.float32)[None, :, None]
        c = cs[:, :, sl]
        d = (c[:, hi] - c[:, lo]) / cnt - u_t[:, :, sl]
        outs.append(jnp.einsum('btc,cd->btd', d, w_groups[g].astype(jnp.float32)))
    y = (jnp.concatenate(outs, axis=-1) + bias) * scale
    return y.astype(u_ext.dtype)


def kv_side(x, norm_kv, w_kvf, b_f):
    B, L, _ = x.shape
    proj = rmsnorm(x, norm_kv) @ w_kvf
    k = proj[..., :D_MODEL].reshape(B, L, N_HEADS, HEAD_DIM)
    v = proj[..., D_MODEL:2 * D_MODEL].reshape(B, L, N_HEADS, HEAD_DIM)
    logf = jax.nn.log_sigmoid((proj[..., 2 * D_MODEL:] + b_f).astype(jnp.float32))
    return k, v, logf


def fox_prompt(q, k, v, logf):
    B, S, H, Dh = q.shape
    nb = S // Q_BLOCK
    F = jnp.cumsum(logf.astype(jnp.float32), axis=1).transpose(0, 2, 1)
    kf, vf = k.astype(jnp.float32), v.astype(jnp.float32)
    qb = (q.astype(jnp.float32) * ATTN_SCALE).reshape(B, nb, Q_BLOCK, H, Dh).transpose(1, 0, 2, 3, 4)
    Fb = F.reshape(B, H, nb, Q_BLOCK).transpose(2, 0, 1, 3)
    kpos = jnp.arange(S)

    def block(args):
        qi, Fi, i = args
        s = jnp.einsum('bqhd,bkhd->bhqk', qi, kf)
        s = s + Fi[..., None] - F[:, :, None, :]
        qpos = i * Q_BLOCK + jnp.arange(Q_BLOCK)
        s = jnp.where(kpos[None, :] <= qpos[:, None], s, -jnp.inf)
        p = jax.nn.softmax(s, axis=-1)
        return jnp.einsum('bhqk,bkhd->bqhd', p, vf)

    o = lax.map(block, (qb, Fb, jnp.arange(nb)))
    return o.transpose(1, 0, 2, 3, 4).reshape(B, S, H, Dh).astype(q.dtype)


def fox_sample(q, k_past, v_past, lf_past, k_new, v_new, lf_new):
    P, T = k_past.shape[1], q.shape[1]
    lf = jnp.concatenate([lf_past.astype(jnp.float32), lf_new.astype(jnp.float32)], axis=1)
    F = jnp.cumsum(lf, axis=1).transpose(0, 2, 1)
    qf = q.astype(jnp.float32) * ATTN_SCALE
    s = jnp.concatenate([
        jnp.einsum('bqhd,bkhd->bhqk', qf, k_past.astype(jnp.float32)),
        jnp.einsum('bqhd,bkhd->bhqk', qf, k_new.astype(jnp.float32))], axis=-1)
    s = s + F[:, :, P:, None] - F[:, :, None, :]
    mask = jnp.arange(P + T)[None, :] <= (P + jnp.arange(T))[:, None]
    s = jnp.where(mask, s, -jnp.inf)
    p = jax.nn.softmax(s, axis=-1)
    o = (jnp.einsum('bhqk,bkhd->bqhd', p[..., :P], v_past.astype(jnp.float32))
         + jnp.einsum('bhqk,bkhd->bqhd', p[..., P:], v_new.astype(jnp.float32)))
    return o.astype(q.dtype)


def setup_inputs(seed: int = 0) -> dict:
    key = jax.random.key(seed)
    ks = jax.random.split(key, 24)
    n_pages = PAST_LEN // PAGE_SIZE
    n_used = DEC_BATCH * n_pages
    n_phys = (5 * n_used + 3) // 4
    f32 = jnp.float32
    nrm = lambda k, shape, s: jax.random.normal(k, shape, f32) * s
    x_prompt = nrm(ks[0], (BATCH, SEQ, D_MODEL), 1.0)
    x_sample = nrm(ks[1], (DEC_BATCH, DEC_SEQ, D_MODEL), 1.0)
    state_pool = nrm(ks[2], (DEC_BATCH, N_A_LAYERS, POOL_STATE, D_MODEL), 1.0)
    cache_k = nrm(ks[3], (n_phys, PAGE_SIZE, N_HEADS, HEAD_DIM), 1.0)
    cache_v = nrm(ks[4], (n_phys, PAGE_SIZE, N_HEADS, HEAD_DIM), 1.0)
    lf_bias = jax.random.uniform(ks[5], (N_HEADS,), f32, 1.0, 6.0)
    cache_logf = jax.nn.log_sigmoid(lf_bias + nrm(ks[6], (n_phys, PAGE_SIZE, N_HEADS), 1.0))
    page_table = jax.random.permutation(ks[7], n_phys)[:n_used].reshape(DEC_BATCH, n_pages).astype(jnp.int32)
    norm_mix = 1.0 + nrm(ks[8], (DEPTH, D_MODEL), 0.02)
    norm_ffn = 1.0 + nrm(ks[9], (DEPTH, D_MODEL), 0.02)
    pool_w = nrm(ks[10], (N_A_LAYERS, N_POOL_GROUPS, POOL_GROUP, POOL_GROUP), POOL_GROUP ** -0.5)
    pool_b = nrm(ks[11], (N_A_LAYERS, D_MODEL), 0.01)
    pool_scale = 1.0 + nrm(ks[12], (N_A_LAYERS, D_MODEL), 0.02)
    norm_kv = 1.0 + nrm(ks[13], (D_MODEL,), 0.02)
    w_kvf = nrm(ks[14], (D_MODEL, 2 * D_MODEL + N_HEADS), D_MODEL ** -0.5)
    b_f = jax.random.uniform(ks[15], (N_HEADS,), f32, 1.0, 6.0)
    w_q = nrm(ks[16], (N_B_LAYERS, D_MODEL, D_MODEL), D_MODEL ** -0.5)
    w_o = nrm(ks[17], (N_B_LAYERS, D_MODEL, D_MODEL), D_MODEL ** -0.5)
    w_gate_up = nrm(ks[18], (DEPTH, D_MODEL, 2 * D_FF), D_MODEL ** -0.5)
    w_down = nrm(ks[19], (DEPTH, D_FF, D_MODEL), D_FF ** -0.5)
    norm_final = 1.0 + nrm(ks[20], (D_MODEL,), 0.02)
    return {"x_prompt": x_prompt, "x_sample": x_sample, "state_pool": state_pool,
            "cache_k": cache_k, "cache_v": cache_v, "cache_logf": cache_logf,
            "page_table": page_table, "norm_mix": norm_mix, "norm_ffn": norm_ffn,
            "pool_w": pool_w, "pool_b": pool_b, "pool_scale": pool_scale,
            "norm_kv": norm_kv, "w_kvf": w_kvf, "b_f": b_f, "w_q": w_q, "w_o": w_o,
            "w_gate_up": w_gate_up, "w_down": w_down, "norm_final": norm_final}


def reference(x_prompt, x_sample, state_pool, cache_k, cache_v, cache_logf, page_table,
              norm_mix, norm_ffn, pool_w, pool_b, pool_scale, norm_kv, w_kvf, b_f,
              w_q, w_o, w_gate_up, w_down, norm_final):
    xp, xs = x_prompt, x_sample
    Bp, S, _ = xp.shape
    Bd, T, _ = xs.shape
    pool_prompt, pool_sample = [], []
    for layer in range(DEPTH):
        hp = rmsnorm(xp, norm_mix[layer])
        hs = rmsnorm(xs, norm_mix[layer])
        if layer < N_A_LAYERS:
            a = layer
            hs_ext = jnp.concatenate([state_pool[:, a].astype(hs.dtype), hs], axis=1)
            xp = xp + pool_mix(hp, 0, pool_w[a], pool_b[a], pool_scale[a])
            xs = xs + pool_mix(hs_ext, POOL_STATE, pool_w[a], pool_b[a], pool_scale[a])
            pool_prompt.append(hp[:, S - POOL_STATE:])
            pool_sample.append(hs_ext[:, hs_ext.shape[1] - POOL_STATE:])
        else:
            b = layer - N_A_LAYERS
            if b == 0:
                k_p, v_p, lf_p = kv_side(xp, norm_kv, w_kvf, b_f)
                k_s, v_s, lf_s = kv_side(xs, norm_kv, w_kvf, b_f)
                n_pages = page_table.shape[1]
                P = n_pages * PAGE_SIZE
                k_past = cache_k[page_table].reshape(Bd, P, N_HEADS, HEAD_DIM)
                v_past = cache_v[page_table].reshape(Bd, P, N_HEADS, HEAD_DIM)
                lf_past = cache_logf[page_table].reshape(Bd, P, N_HEADS)
            qp = (hp @ w_q[b]).reshape(Bp, S, N_HEADS, HEAD_DIM)
            qs = (hs @ w_q[b]).reshape(Bd, T, N_HEADS, HEAD_DIM)
            op = fox_prompt(qp, k_p, v_p, lf_p).reshape(Bp, S, D_MODEL)
            os_ = fox_sample(qs, k_past, v_past, lf_past, k_s, v_s, lf_s).reshape(Bd, T, D_MODEL)
            xp = xp + op @ w_o[b]
            xs = xs + os_ @ w_o[b]
        xp = xp + swiglu(rmsnorm(xp, norm_ffn[layer]), w_gate_up[layer], w_down[layer])
        xs = xs + swiglu(rmsnorm(xs, norm_ffn[layer]), w_gate_up[layer], w_down[layer])
    y_prompt = rmsnorm(xp, norm_final)
    y_sample = rmsnorm(xs, norm_final)
    pool_state_prompt = jnp.stack(pool_prompt, axis=1)
    pool_state_sample = jnp.stack(pool_sample, axis=1)
    return (y_prompt, y_sample, pool_state_prompt, pool_state_sample,
            k_p, v_p, lf_p, k_s, v_s, lf_s)
```

```python
import functools

import jax
import jax.numpy as jnp
from jax import lax
from jax.experimental import pallas as pl
from jax.experimental.pallas import tpu as pltpu

D_MODEL = 1024
N_HEADS = 16
HEAD_DIM = 64
D_FF = 2816
POOL_WINDOWS = (2, 4, 8, 16)
POOL_GROUP = D_MODEL // len(POOL_WINDOWS)
POOL_STATE = max(POOL_WINDOWS) - 1
EPS = 1e-6
ATTN_SCALE = HEAD_DIM ** -0.5
PAGE_SIZE = 128

LANES = 128
SUBLANES = 8
HIST = 2 * SUBLANES
N_SPLIT = 3
MASKED = -1e30
VMEM_LIMIT_BYTES = 56 * 1024 * 1024

TILE_TOKENS = 256
ATTN_TQ = 512
ATTN_TK = 256
PAGES_PER_STEP = 8

F32 = jnp.float32
BF16 = jnp.bfloat16


def _dot(a, b):
    return jnp.dot(a, b, preferred_element_type=F32)


def _rmsnorm(x, g):
    ms = jnp.mean(x * x, axis=-1, keepdims=True)
    return x * lax.rsqrt(ms + EPS) * g


def _log_sigmoid(x):
    return jnp.minimum(x, 0.0) - jnp.log1p(jnp.exp(-jnp.abs(x)))


def _ffn(x, g_norm, wgu_ref, wd_ref):
    h = _rmsnorm(x, g_norm).astype(BF16)
    gu = _dot(h, wgu_ref[...])
    g = gu[:, :D_FF]
    u = gu[:, D_FF:]
    a = g * jax.nn.sigmoid(g) * u
    return x + _dot(a.astype(BF16), wd_ref[...])


def _split_bf16(x):
    pieces = []
    r = x
    for _ in range(N_SPLIT):
        p = r.astype(BF16)
        pieces.append(p)
        r = r - p.astype(F32)
    return pieces


def _lower_tri(n):
    r = lax.broadcasted_iota(jnp.int32, (n, n), 0)
    c = lax.broadcasted_iota(jnp.int32, (n, n), 1)
    return jnp.where(r >= c, 1.0, 0.0).astype(BF16)


def _prefix_sum_rows(x, tri):
    out = None
    for p in _split_bf16(x):
        t = _dot(tri, p)
        out = t if out is None else out + t
    return out


def _pool_mix(cur_fn, shifted_fn, inv_cnt_fn, pw_ref):
    outs = []
    for g, w in enumerate(POOL_WINDOWS):
        cur = cur_fn(g)
        s = cur
        for j in range(1, w):
            s = s + shifted_fn(g, j)
        d = s * inv_cnt_fn(w) - cur
        outs.append(_dot(d.astype(BF16), pw_ref[g]))
    return jnp.concatenate(outs, axis=-1)


def _layer0_prompt_kernel(x_ref, nm_ref, nf_ref, pb_ref, ps_ref, pw_ref, wgu_ref, wd_ref,
                          x2_ref, pstate_ref, hext_ref, *, tile):
    i = pl.program_id(1)

    @pl.when(i == 0)
    def _():
        hext_ref[0:HIST, :] = jnp.zeros((HIST, D_MODEL), F32)

    x = x_ref[0]
    h = _rmsnorm(x, nm_ref[...])
    hext_ref[HIST:HIST + tile, :] = h
    pos = lax.broadcasted_iota(jnp.int32, (tile, 1), 0) + i * tile

    def lanes(g):
        return slice(g * POOL_GROUP, (g + 1) * POOL_GROUP)

    y = _pool_mix(
        lambda g: h[:, lanes(g)],
        lambda g, j: hext_ref[HIST - j:HIST - j + tile, lanes(g)],
        lambda w: 1.0 / jnp.minimum(pos + 1, w).astype(F32),
        pw_ref)
    x1 = x + (y + pb_ref[...]) * ps_ref[...]
    x2_ref[0] = _ffn(x1, nf_ref[...], wgu_ref, wd_ref)

    hext_ref[0:HIST, :] = hext_ref[tile:tile + HIST, :]

    @pl.when(i == pl.num_programs(1) - 1)
    def _():
        pstate_ref[0, 0] = hext_ref[HIST - POOL_STATE:HIST, :]


def _kvq_prompt_kernel(x2_ref, nkv_ref, nm1_ref, bf_ref, wk_ref, wv_ref, wf_ref, wq_ref, place_ref,
                       k_ref, v_ref, lf_ref, qs_ref, ks_ref, vs_ref, carry_ref, *, tile):
    i = pl.program_id(1)

    @pl.when(i == 0)
    def _():
        carry_ref[...] = jnp.zeros(carry_ref.shape, F32)

    x2 = x2_ref[0]
    hk = _rmsnorm(x2, nkv_ref[...]).astype(BF16)
    k = _dot(hk, wk_ref[...])
    v = _dot(hk, wv_ref[...])
    lf = _log_sigmoid(_dot(hk, wf_ref[...]) + bf_ref[...])
    k_ref[0] = k
    v_ref[0] = v
    lf_ref[0] = lf[:, :N_HEADS]

    lane = lax.broadcasted_iota(jnp.int32, (tile, LANES), 1)
    lf = jnp.where(lane < N_HEADS, lf, 0.0)
    fcum = _prefix_sum_rows(lf, _lower_tri(tile)) + carry_ref[0:1, :]
    carry_ref[0:1, :] = fcum[tile - 1:tile, :]

    packed = None
    for n, p in enumerate(_split_bf16(-fcum)):
        p = p.astype(F32)
        p = p if n == 0 else pltpu.roll(p, n * N_HEADS, 1)
        packed = p if packed is None else packed + p
    bias = _dot(packed.astype(BF16), place_ref[...])

    q = _dot(_rmsnorm(x2, nm1_ref[...]).astype(BF16), wq_ref[...])

    is_data = lane < HEAD_DIM
    q_fill = jnp.where(lane < HEAD_DIM + N_SPLIT, 1.0, 0.0)
    v_fill = jnp.where(lane == HEAD_DIM, 1.0, 0.0)
    for pair in range(N_HEADS // 2):
        cols = slice(pair * LANES, (pair + 1) * LANES)
        for odd in range(2):
            head = 2 * pair + odd
            slab = slice(head * LANES, (head + 1) * LANES)

            def data(a):
                a = a[:, cols]
                return pltpu.roll(a, HEAD_DIM, 1) if odd else a

            qs_ref[0, :, slab] = jnp.where(is_data, data(q), q_fill).astype(BF16)
            ks_ref[0, :, slab] = jnp.where(is_data, data(k), bias[:, slab]).astype(BF16)
            vs_ref[0, :, slab] = jnp.where(is_data, data(v), v_fill).astype(BF16)


def _softmax_step(s, vb, m_ref, acc_ref, rows):
    m_old = m_ref[rows, :]
    m_new = jnp.maximum(m_old, jnp.max(s, axis=1, keepdims=True))
    p = jnp.exp(s - m_new)
    alpha = jnp.exp(m_old - m_new)
    acc_ref[rows, :] = acc_ref[rows, :] * alpha + _dot(p.astype(BF16), vb)
    m_ref[rows, :] = m_new


def _qk(q, kb):
    return lax.dot_general(q, kb, (((1,), (1,)), ((), ())), preferred_element_type=F32)


def _attn_prompt_kernel(q_ref, k_ref, v_ref, o_ref, m_ref, acc_ref, *, tq, tk):
    i = pl.program_id(2)
    outs = []
    for hh in range(2):
        slab = slice(hh * LANES, (hh + 1) * LANES)
        q = q_ref[0, :, slab]
        m_ref[...] = jnp.full(m_ref.shape, MASKED, F32)
        acc_ref[...] = jnp.zeros(acc_ref.shape, F32)

        def body(j, carry):
            start = pl.multiple_of(j * tk, tk)
            kb = k_ref[0, pl.ds(start, tk), slab]
            vb = v_ref[0, pl.ds(start, tk), slab]
            _softmax_step(_qk(q, kb), vb, m_ref, acc_ref, slice(None))
            return carry

        lax.fori_loop(0, i * (tq // tk), body, 0)

        for d in range(tq // tk):
            r0 = d * tk
            start = pl.multiple_of(i * tq + r0, tk)
            kb = k_ref[0, pl.ds(start, tk), slab]
            vb = v_ref[0, pl.ds(start, tk), slab]
            s = _qk(q[r0:, :], kb)
            rr = lax.broadcasted_iota(jnp.int32, s.shape, 0)
            cc = lax.broadcasted_iota(jnp.int32, s.shape, 1)
            s = jnp.where(cc <= rr, s, MASKED)
            _softmax_step(s, vb, m_ref, acc_ref, slice(r0, tq))

        acc = acc_ref[...]
        outs.append(acc / acc[:, HEAD_DIM:HEAD_DIM + 1])
    lane = lax.broadcasted_iota(jnp.int32, (tq, LANES), 1)
    o_ref[0] = jnp.where(lane < HEAD_DIM, outs[0], pltpu.roll(outs[1], HEAD_DIM, 1)).astype(BF16)


def _post_kernel(o_ref, x2_ref, wo_ref, nf_ref, wgu_ref, wd_ref, nfin_ref, y_ref):
    x3 = x2_ref[0] + _dot(o_ref[0].astype(BF16), wo_ref[...])
    x4 = _ffn(x3, nf_ref[...], wgu_ref, wd_ref)
    y_ref[0] = _rmsnorm(x4, nfin_ref[...])


def _pre_sample_kernel(xs_ref, st_ref, nm0_ref, nf0_ref, pb_ref, ps_ref, pw_ref, wgu_ref, wd_ref,
                       nkv_ref, nm1_ref, bf_ref, wk_ref, wv_ref, wf_ref, wq_ref,
                       xs2_ref, pst_ref, k_ref, v_ref, lf_ref, q_ref):
    x = xs_ref[...]
    h = _rmsnorm(x, nm0_ref[...])
    for r in range(POOL_STATE - 1):
        pst_ref[r] = st_ref[r + 1]
    pst_ref[POOL_STATE - 1] = h

    def lanes(g):
        return slice(g * POOL_GROUP, (g + 1) * POOL_GROUP)

    y = _pool_mix(
        lambda g: h[:, lanes(g)],
        lambda g, j: st_ref[POOL_STATE - j, :, lanes(g)],
        lambda w: 1.0 / w,
        pw_ref)
    x1 = x + (y + pb_ref[...]) * ps_ref[...]
    x2 = _ffn(x1, nf0_ref[...], wgu_ref, wd_ref)
    xs2_ref[...] = x2
    hk = _rmsnorm(x2, nkv_ref[...]).astype(BF16)
    k_ref[...] = _dot(hk, wk_ref[...])
    v_ref[...] = _dot(hk, wv_ref[...])
    lf_ref[...] = _log_sigmoid(_dot(hk, wf_ref[...]) + bf_ref[...])
    q_ref[...] = _dot(_rmsnorm(x2, nm1_ref[...]).astype(BF16), wq_ref[...])


def _attn_sample_kernel(pt_ref, q_ref, kn_ref, vn_ref, lfn_ref, *rest, npg):
    del pt_ref
    k_refs = rest[:npg]
    v_refs = rest[npg:2 * npg]
    lf_refs = rest[2 * npg:3 * npg]
    o_ref = rest[3 * npg]
    qbd_ref, m_ref, l_ref, acc_ref, carry_ref, pad_ref = rest[3 * npg + 1:]
    j = pl.program_id(1)

    head_of_lane = lax.broadcasted_iota(jnp.int32, (LANES, D_MODEL), 1) // HEAD_DIM
    row = lax.broadcasted_iota(jnp.int32, (LANES, D_MODEL), 0)

    @pl.when(j == 0)
    def _():
        qbd_t = jnp.where(head_of_lane == row, jnp.broadcast_to(q_ref[0], (LANES, D_MODEL)), 0.0)
        qbd_ref[...] = qbd_t.T.astype(BF16)
        m_ref[...] = jnp.full(m_ref.shape, MASKED, F32)
        l_ref[...] = jnp.zeros(l_ref.shape, F32)
        acc_ref[...] = jnp.zeros(acc_ref.shape, F32)
        carry_ref[...] = jnp.zeros(carry_ref.shape, F32)
        pad_ref[...] = jnp.zeros(pad_ref.shape, F32)

    def update(logits_t, v_pages):
        u = jnp.concatenate(logits_t, axis=1) if len(logits_t) > 1 else logits_t[0]
        m_old = m_ref[...]
        m_new = jnp.maximum(m_old, jnp.max(u, axis=1, keepdims=True))
        p = jnp.exp(u - m_new)
        alpha = jnp.exp(m_old - m_new)
        l_ref[...] = l_ref[...] * alpha + jnp.sum(p, axis=1, keepdims=True)
        pv = None
        for n, vp in enumerate(v_pages):
            t = _dot(p[:, n * PAGE_SIZE:(n + 1) * PAGE_SIZE].astype(BF16), vp)
            pv = t if pv is None else pv + t
        acc_ref[...] = acc_ref[...] * alpha + pv
        m_ref[...] = m_new

    tri = _lower_tri(PAGE_SIZE)
    qbd = qbd_ref[...]
    carry = carry_ref[0:1, :]
    logits_t, v_pages = [], []
    for n in range(npg):
        s = _dot(k_refs[n][0].astype(BF16), qbd)
        pad_ref[:, 0:N_HEADS] = lf_refs[n][0]
        fcum = _prefix_sum_rows(pad_ref[...], tri) + carry
        carry = fcum[PAGE_SIZE - 1:PAGE_SIZE, :]
        logits_t.append((s - fcum).T[0:N_HEADS, :])
        v_pages.append(v_refs[n][0].astype(BF16))
    carry_ref[0:1, :] = carry
    update(logits_t, v_pages)

    @pl.when(j == pl.num_programs(1) - 1)
    def _():
        kn = jnp.broadcast_to(kn_ref[0], (SUBLANES, D_MODEL)).astype(BF16)
        s_new = _dot(kn, qbd)[0:1, :] - (carry + lfn_ref[0])
        prow = lax.broadcasted_iota(jnp.int32, (PAGE_SIZE, LANES), 0)
        u_new = jnp.where(prow == 0, jnp.broadcast_to(s_new, (PAGE_SIZE, LANES)), MASKED)
        v_new = jnp.where(row == 0, jnp.broadcast_to(vn_ref[0], (PAGE_SIZE, D_MODEL)), 0.0)
        update([u_new.T[0:N_HEADS, :]], [v_new.astype(BF16)])
        o = acc_ref[...] / l_ref[...]
        own = (lax.broadcasted_iota(jnp.int32, o.shape, 1) // HEAD_DIM
               == lax.broadcasted_iota(jnp.int32, o.shape, 0))
        o_ref[0] = jnp.sum(jnp.where(own, o, 0.0), axis=0, keepdims=True)


def _const_spec(shape):
    nd = len(shape)
    return pl.BlockSpec(shape, lambda *_: (0,) * nd, pipeline_mode=pl.Buffered(1))


def _params(*sem):
    return pltpu.CompilerParams(dimension_semantics=sem, vmem_limit_bytes=VMEM_LIMIT_BYTES)


def _row(a):
    return a.reshape(1, -1).astype(F32)


def _placement():
    src = jnp.arange(LANES)[:, None]
    dst = jnp.arange(N_HEADS * LANES)[None, :]
    head, off = dst // LANES, dst % LANES - HEAD_DIM
    hit = (off >= 0) & (off < N_SPLIT) & (src == off * N_HEADS + head)
    return hit.astype(BF16)


def _layer0_prompt(x, nm, nf, pb, ps, pw, wgu, wd, tile):
    b, s, d = x.shape
    tok = pl.BlockSpec((1, tile, d), lambda bi, i: (bi, i, 0))
    return pl.pallas_call(
        functools.partial(_layer0_prompt_kernel, tile=tile),
        grid=(b, s // tile),
        in_specs=[tok] + [_const_spec(a.shape) for a in (nm, nf, pb, ps, pw, wgu, wd)],
        out_specs=[tok, pl.BlockSpec((1, 1, POOL_STATE, d), lambda bi, i: (bi, 0, 0, 0))],
        out_shape=[jax.ShapeDtypeStruct((b, s, d), F32),
                   jax.ShapeDtypeStruct((b, 1, POOL_STATE, d), F32)],
        scratch_shapes=[pltpu.VMEM((HIST + tile, d), F32)],
        compiler_params=_params("arbitrary", "arbitrary"),
        name="layer0_prompt",
    )(x, nm, nf, pb, ps, pw, wgu, wd)


def _kvq_prompt(x2, nkv, nm1, bf, wk, wv, wf, wq, tile):
    b, s, d = x2.shape
    place = _placement()
    tok = pl.BlockSpec((1, tile, d), lambda bi, i: (bi, i, 0))
    slabs = pl.BlockSpec((1, tile, N_HEADS * LANES), lambda bi, i: (bi, i, 0))
    consts = (nkv, nm1, bf, wk, wv, wf, wq, place)
    return pl.pallas_call(
        functools.partial(_kvq_prompt_kernel, tile=tile),
        grid=(b, s // tile),
        in_specs=[tok] + [_const_spec(a.shape) for a in consts],
        out_specs=[tok, tok, pl.BlockSpec((1, tile, N_HEADS), lambda bi, i: (bi, i, 0)),
                   slabs, slabs, slabs],
        out_shape=[jax.ShapeDtypeStruct((b, s, d), F32), jax.ShapeDtypeStruct((b, s, d), F32),
                   jax.ShapeDtypeStruct((b, s, N_HEADS), F32)]
                  + [jax.ShapeDtypeStruct((b, s, N_HEADS * LANES), BF16)] * 3,
        scratch_shapes=[pltpu.VMEM((SUBLANES, LANES), F32)],
        compiler_params=_params("arbitrary", "arbitrary"),
        name="kvq_prompt",
    )(x2, *consts)


def _attn_prompt(qs, ks, vs, tq, tk):
    b, s, _ = qs.shape
    pair = 2 * LANES
    kv = pl.BlockSpec((1, s, pair), lambda bi, hg, i: (bi, 0, hg))
    return pl.pallas_call(
        functools.partial(_attn_prompt_kernel, tq=tq, tk=tk),
        grid=(b, N_HEADS // 2, s // tq),
        in_specs=[pl.BlockSpec((1, tq, pair), lambda bi, hg, i: (bi, i, hg)), kv, kv],
        out_specs=pl.BlockSpec((1, tq, LANES), lambda bi, hg, i: (bi, i, hg)),
        out_shape=jax.ShapeDtypeStruct((b, s, D_MODEL), BF16),
        scratch_shapes=[pltpu.VMEM((tq, 1), F32), pltpu.VMEM((tq, LANES), F32)],
        compiler_params=_params("parallel", "parallel", "arbitrary"),
        name="attn_prompt",
    )(qs, ks, vs)


def _post(o, x2, wo, nf, wgu, wd, nfin, tile):
    b, s, d = x2.shape
    tok = pl.BlockSpec((1, tile, d), lambda bi, i: (bi, i, 0))
    consts = (wo, nf, wgu, wd, nfin)
    return pl.pallas_call(
        _post_kernel,
        grid=(b, s // tile),
        in_specs=[tok, tok] + [_const_spec(a.shape) for a in consts],
        out_specs=tok,
        out_shape=jax.ShapeDtypeStruct((b, s, d), F32),
        compiler_params=_params("parallel", "parallel"),
        name="post",
    )(o, x2, *consts)


def _pre_sample(xs, st, consts):
    n, d = xs.shape
    out_shape = [jax.ShapeDtypeStruct((n, d), F32), jax.ShapeDtypeStruct((POOL_STATE, n, d), F32),
                 jax.ShapeDtypeStruct((n, d), F32), jax.ShapeDtypeStruct((n, d), F32),
                 jax.ShapeDtypeStruct((n, LANES), F32), jax.ShapeDtypeStruct((n, d), F32)]
    return pl.pallas_call(
        _pre_sample_kernel,
        grid=(1,),
        in_specs=[_const_spec(a.shape) for a in (xs, st) + consts],
        out_specs=[pl.BlockSpec(o.shape, lambda i, nd=len(o.shape): (0,) * nd) for o in out_shape],
        out_shape=out_shape,
        compiler_params=_params("arbitrary"),
        name="pre_sample",
    )(xs, st, *consts)


def _attn_sample(page_table, q, kn, vn, lfn, cache_k, cache_v, cache_lf, npg):
    n, n_pages = page_table.shape
    one = lambda w: pl.BlockSpec((1, 1, w), lambda bi, j, pt: (bi, 0, 0))

    def paged(w, p):
        return pl.BlockSpec((1, PAGE_SIZE, w), lambda bi, j, pt: (pt[bi, j * npg + p], 0, 0))

    in_specs = ([one(D_MODEL), one(D_MODEL), one(D_MODEL), one(LANES)]
                + [paged(D_MODEL, p) for p in range(npg)]
                + [paged(D_MODEL, p) for p in range(npg)]
                + [paged(N_HEADS, p) for p in range(npg)])
    return pl.pallas_call(
        functools.partial(_attn_sample_kernel, npg=npg),
        grid_spec=pltpu.PrefetchScalarGridSpec(
            num_scalar_prefetch=1,
            grid=(n, n_pages // npg),
            in_specs=in_specs,
            out_specs=one(D_MODEL),
            scratch_shapes=[pltpu.VMEM((D_MODEL, LANES), BF16),
                            pltpu.VMEM((N_HEADS, 1), F32), pltpu.VMEM((N_HEADS, 1), F32),
                            pltpu.VMEM((N_HEADS, D_MODEL), F32),
                            pltpu.VMEM((SUBLANES, LANES), F32),
                            pltpu.VMEM((PAGE_SIZE, LANES), F32)]),
        out_shape=jax.ShapeDtypeStruct((n, 1, D_MODEL), F32),
        compiler_params=_params("parallel", "arbitrary"),
        name="attn_sample",
    )(page_table, q, kn, vn, lfn, *([cache_k] * npg), *([cache_v] * npg), *([cache_lf] * npg))


def kernel(x_prompt, x_sample, state_pool, cache_k, cache_v, cache_logf, page_table, norm_mix, norm_ffn,
           pool_w, pool_b, pool_scale, norm_kv, w_kvf, b_f, w_q, w_o, w_gate_up, w_down, norm_final):
    bp, s, d = x_prompt.shape
    bd = x_sample.shape[0]
    n_phys = cache_k.shape[0]

    pw = pool_w[0].astype(BF16)
    wgu = w_gate_up.astype(BF16)
    wd = w_down.astype(BF16)
    wk = w_kvf[:, :d].astype(BF16)
    wv = w_kvf[:, d:2 * d].astype(BF16)
    wf = jnp.pad(w_kvf[:, 2 * d:], ((0, 0), (0, LANES - N_HEADS))).astype(BF16)
    bf = jnp.pad(b_f, (0, LANES - N_HEADS)).reshape(1, LANES).astype(F32)
    wq = (w_q[0] * ATTN_SCALE).astype(BF16)
    wo = w_o[0].astype(BF16)
    nm0, nm1 = _row(norm_mix[0]), _row(norm_mix[1])
    nf0, nf1 = _row(norm_ffn[0]), _row(norm_ffn[1])
    pb, ps = _row(pool_b[0]), _row(pool_scale[0])
    nkv, nfin = _row(norm_kv), _row(norm_final)

    x2p, pool_state_prompt = _layer0_prompt(x_prompt, nm0, nf0, pb, ps, pw, wgu[0], wd[0], TILE_TOKENS)
    k_p, v_p, lf_p, qs, ks, vs = _kvq_prompt(x2p, nkv, nm1, bf, wk, wv, wf, wq, TILE_TOKENS)
    o_p = _attn_prompt(qs, ks, vs, ATTN_TQ, ATTN_TK)
    y_prompt = _post(o_p, x2p, wo, nf1, wgu[1], wd[1], nfin, TILE_TOKENS)

    xs = x_sample.reshape(bd, d)
    st = jnp.transpose(state_pool[:, 0], (1, 0, 2))
    pre_consts = (nm0, nf0, pb, ps, pw, wgu[0], wd[0], nkv, nm1, bf, wk, wv, wf, wq)
    xs2, pst, k_s, v_s, lf_s, q_s = _pre_sample(xs, st, pre_consts)
    as3 = lambda a: a.reshape(bd, 1, -1)
    o_s = _attn_sample(page_table, as3(q_s), as3(k_s), as3(v_s), as3(lf_s),
                       cache_k.reshape(n_phys, PAGE_SIZE, d), cache_v.reshape(n_phys, PAGE_SIZE, d),
                       cache_logf, PAGES_PER_STEP)
    y_sample = _post(o_s.reshape(1, bd, d), xs2.reshape(1, bd, d), wo, nf1, wgu[1], wd[1], nfin, bd)

    heads = lambda a, n: a.reshape(a.shape[0], n, N_HEADS, HEAD_DIM)
    return (y_prompt, y_sample.reshape(bd, 1, d),
            pool_state_prompt, jnp.transpose(pst, (1, 0, 2)).reshape(bd, 1, POOL_STATE, d),
            heads(k_p, s), heads(v_p, s), lf_p,
            heads(k_s, 1), heads(v_s, 1), lf_s[:, :N_HEADS].reshape(bd, 1, N_HEADS))
```

```python
import functools

import jax
import jax.numpy as jnp
from jax import lax
from jax.experimental import pallas as pl
from jax.experimental.pallas import tpu as pltpu

D_MODEL = 1024
N_HEADS = 16
HEAD_DIM = 64
D_FF = 2816
POOL_WINDOWS = (2, 4, 8, 16)
POOL_GROUP = D_MODEL // len(POOL_WINDOWS)
POOL_STATE = max(POOL_WINDOWS) - 1
EPS = 1e-6
ATTN_SCALE = HEAD_DIM ** -0.5
PAGE_SIZE = 128

LANES = 128
SUBLANES = 8
HIST = 2 * SUBLANES
N_SPLIT = 3
MASKED = -1e30
LOG2E = 1.4426950408889634
Q_ROWS = LANES
FILL_ROWS = 16
V_ROWS = HEAD_DIM + FILL_ROWS
VMEM_LIMIT_BYTES = 56 * 1024 * 1024

TILE_TOKENS = 256
ATTN_TQ = 512
ATTN_TK = 256
PAGES_PER_STEP = 8

F32 = jnp.float32
BF16 = jnp.bfloat16


def _dot(a, b):
    return jnp.dot(a, b, preferred_element_type=F32)


def _rmsnorm(x, g):
    ms = jnp.mean(x * x, axis=-1, keepdims=True)
    return x * lax.rsqrt(ms + EPS) * g


def _log_sigmoid(x):
    return jnp.minimum(x, 0.0) - jnp.log1p(jnp.exp(-jnp.abs(x)))


def _ffn(x, g_norm, wgu_ref, wd_ref):
    h = _rmsnorm(x, g_norm).astype(BF16)
    gu = _dot(h, wgu_ref[...])
    g = gu[:, :D_FF]
    u = gu[:, D_FF:]
    a = g * jax.nn.sigmoid(g) * u
    return x + _dot(a.astype(BF16), wd_ref[...])


def _split_bf16(x):
    pieces = []
    r = x
    for _ in range(N_SPLIT):
        p = r.astype(BF16)
        pieces.append(p)
        r = r - p.astype(F32)
    return pieces


def _lower_tri(n):
    r = lax.broadcasted_iota(jnp.int32, (n, n), 0)
    c = lax.broadcasted_iota(jnp.int32, (n, n), 1)
    return jnp.where(r >= c, 1.0, 0.0).astype(BF16)


def _prefix_sum_rows(x, tri):
    out = None
    for p in _split_bf16(x):
        t = _dot(tri, p)
        out = t if out is None else out + t
    return out


def _pool_mix(cur_fn, shifted_fn, inv_cnt_fn, pw_ref):
    outs = []
    for g, w in enumerate(POOL_WINDOWS):
        cur = cur_fn(g)
        s = cur
        for j in range(1, w):
            s = s + shifted_fn(g, j)
        d = s * inv_cnt_fn(w) - cur
        outs.append(_dot(d.astype(BF16), pw_ref[g]))
    return jnp.concatenate(outs, axis=-1)


def _layer0_prompt_kernel(x_ref, nm_ref, nf_ref, pb_ref, ps_ref, pw_ref, wgu_ref, wd_ref,
                          x2_ref, pstate_ref, hext_ref, *, tile):
    i = pl.program_id(1)

    @pl.when(i == 0)
    def _():
        hext_ref[0:HIST, :] = jnp.zeros((HIST, D_MODEL), F32)

    x = x_ref[0]
    h = _rmsnorm(x, nm_ref[...])
    hext_ref[HIST:HIST + tile, :] = h
    pos = lax.broadcasted_iota(jnp.int32, (tile, 1), 0) + i * tile

    def lanes(g):
        return slice(g * POOL_GROUP, (g + 1) * POOL_GROUP)

    y = _pool_mix(
        lambda g: h[:, lanes(g)],
        lambda g, j: hext_ref[HIST - j:HIST - j + tile, lanes(g)],
        lambda w: 1.0 / jnp.minimum(pos + 1, w).astype(F32),
        pw_ref)
    x1 = x + (y + pb_ref[...]) * ps_ref[...]
    x2_ref[0] = _ffn(x1, nf_ref[...], wgu_ref, wd_ref)

    hext_ref[0:HIST, :] = hext_ref[tile:tile + HIST, :]

    @pl.when(i == pl.num_programs(1) - 1)
    def _():
        pstate_ref[0, 0] = hext_ref[HIST - POOL_STATE:HIST, :]


def _kvq_prompt_kernel(x2_ref, nkv_ref, nm1_ref, bf_ref, wk_ref, wv_ref, wf_ref, wq_ref, place_ref,
                       k_ref, v_ref, lf_ref, ks_ref, qt_ref, vt_ref, carry_ref, q_ref, *, tile):
    i = pl.program_id(1)

    @pl.when(i == 0)
    def _():
        carry_ref[...] = jnp.zeros(carry_ref.shape, F32)

    x2 = x2_ref[0]
    hk = _rmsnorm(x2, nkv_ref[...]).astype(BF16)
    k = _dot(hk, wk_ref[...])
    v = _dot(hk, wv_ref[...])
    lf = _log_sigmoid(_dot(hk, wf_ref[...]) + bf_ref[...])
    k_ref[0] = k
    v_ref[0] = v
    lf_ref[0] = lf[:, :N_HEADS]

    lane = lax.broadcasted_iota(jnp.int32, (tile, LANES), 1)
    lf = jnp.where(lane < N_HEADS, lf, 0.0)
    fcum = _prefix_sum_rows(lf, _lower_tri(tile)) + carry_ref[0:1, :]
    carry_ref[0:1, :] = fcum[tile - 1:tile, :]

    packed = None
    for n, p in enumerate(_split_bf16(-LOG2E * fcum)):
        p = p.astype(F32)
        p = p if n == 0 else pltpu.roll(p, n * N_HEADS, 1)
        packed = p if packed is None else packed + p
    bias = _dot(packed.astype(BF16), place_ref[...])

    is_data = lane < HEAD_DIM
    for pair in range(N_HEADS // 2):
        cols = k[:, pair * LANES:(pair + 1) * LANES]
        for odd in range(2):
            slab = slice((2 * pair + odd) * LANES, (2 * pair + odd + 1) * LANES)
            data = pltpu.roll(cols, HEAD_DIM, 1) if odd else cols
            ks_ref[0, :, slab] = jnp.where(is_data, data, bias[:, slab]).astype(BF16)

    q_ref[...] = LOG2E * _dot(_rmsnorm(x2, nm1_ref[...]).astype(BF16), wq_ref[...])
    qt = q_ref[...].T
    vt = v_ref[0].T
    fill_row = lax.broadcasted_iota(jnp.int32, (FILL_ROWS, tile), 0)
    q_fill = jnp.where(fill_row < N_SPLIT, 1.0, 0.0).astype(BF16)
    v_fill = jnp.where(fill_row == 0, 1.0, 0.0).astype(BF16)
    q_zero = jnp.zeros((Q_ROWS - HEAD_DIM - FILL_ROWS, tile), BF16)
    for h in range(N_HEADS):
        rows = slice(h * HEAD_DIM, (h + 1) * HEAD_DIM)
        q0, v0 = h * Q_ROWS, h * V_ROWS
        qt_ref[0, q0:q0 + HEAD_DIM, :] = qt[rows, :].astype(BF16)
        qt_ref[0, q0 + HEAD_DIM:q0 + HEAD_DIM + FILL_ROWS, :] = q_fill
        qt_ref[0, q0 + HEAD_DIM + FILL_ROWS:q0 + Q_ROWS, :] = q_zero
        vt_ref[0, 0, v0:v0 + HEAD_DIM, :] = vt[rows, :].astype(BF16)
        vt_ref[0, 0, v0 + HEAD_DIM:v0 + V_ROWS, :] = v_fill


def _attn_prompt_kernel(qt_ref, ks_ref, vt_ref, o_ref, m_ref, acc_ref, s_ref, p_ref, a_ref, *, tq, tk):
    i = pl.program_id(2)
    heads = range(2)
    per_q = tq // tk
    units = [(d, hh) for d in range(per_q) for hh in heads]
    qts = [qt_ref[0, hh * Q_ROWS:(hh + 1) * Q_ROWS, :] for hh in heads]

    def scores(kblk, hh, cols=slice(None)):
        start = pl.multiple_of(kblk * tk, tk)
        return _dot(ks_ref[0, pl.ds(start, tk), hh * LANES:(hh + 1) * LANES], qts[hh][:, cols])

    def softmax(s, hh, cols=slice(None)):
        m_old = m_ref[hh, :, cols]
        m_new = jnp.maximum(m_old, jnp.max(s, axis=0, keepdims=True))
        m_ref[hh, :, cols] = m_new
        return jnp.exp2(s - m_new).astype(BF16), jnp.exp2(m_old - m_new)

    def accumulate(p, alpha, kblk, hh, cols=slice(None)):
        vb = vt_ref[0, kblk, hh * V_ROWS:(hh + 1) * V_ROWS, :]
        acc_ref[hh, :, cols] = acc_ref[hh, :, cols] * alpha + _dot(vb, p)

    for hh in heads:
        m_ref[hh] = jnp.full((1, tq), MASKED, F32)
        acc_ref[hh] = jnp.zeros((V_ROWS, tq), F32)

    def value_stage(jj):
        for u, (d, hh) in enumerate(units):
            accumulate(p_ref[u], a_ref[u], jj * per_q + d, hh)

    def softmax_stage():
        for u, (d, hh) in enumerate(units):
            p_ref[u], a_ref[u] = softmax(s_ref[u], hh)

    def score_stage(jj):
        for u, (d, hh) in enumerate(units):
            s_ref[u] = scores(jj * per_q + d, hh)

    for u in range(len(units)):
        p_ref[u] = jnp.zeros((tk, tq), BF16)
        a_ref[u] = jnp.ones((1, tq), F32)
    score_stage(0)

    def body(jj, carry):
        value_stage(jnp.maximum(jj - 1, 0))
        softmax_stage()
        score_stage(jnp.minimum(jj + 1, i - 1))
        return carry

    lax.fori_loop(0, i, body, 0)
    value_stage(jnp.maximum(i - 1, 0))

    for d in range(per_q):
        cols = slice(d * tk, tq)
        kb = i * per_q + d
        for hh in heads:
            s = scores(kb, hh, cols)
            key = lax.broadcasted_iota(jnp.int32, s.shape, 0)
            qry = lax.broadcasted_iota(jnp.int32, s.shape, 1)
            p, alpha = softmax(jnp.where(key <= qry, s, MASKED), hh, cols)
            accumulate(p, alpha, kb, hh, cols)

    outs = []
    for hh in heads:
        acc = acc_ref[hh]
        outs.append(acc[0:HEAD_DIM, :] / acc[HEAD_DIM:HEAD_DIM + 1, :])
    o_ref[0] = jnp.concatenate(outs, axis=0).T.astype(BF16)


def _post_kernel(o_ref, x2_ref, wo_ref, nf_ref, wgu_ref, wd_ref, nfin_ref, y_ref):
    x3 = x2_ref[0] + _dot(o_ref[0].astype(BF16), wo_ref[...])
    x4 = _ffn(x3, nf_ref[...], wgu_ref, wd_ref)
    y_ref[0] = _rmsnorm(x4, nfin_ref[...])


def _pre_sample_kernel(xs_ref, st_ref, nm0_ref, nf0_ref, pb_ref, ps_ref, pw_ref, wgu_ref, wd_ref,
                       nkv_ref, nm1_ref, bf_ref, wk_ref, wv_ref, wf_ref, wq_ref,
                       xs2_ref, pst_ref, k_ref, v_ref, lf_ref, q_ref):
    x = xs_ref[...]
    h = _rmsnorm(x, nm0_ref[...])
    for r in range(POOL_STATE - 1):
        pst_ref[r] = st_ref[r + 1]
    pst_ref[POOL_STATE - 1] = h

    def lanes(g):
        return slice(g * POOL_GROUP, (g + 1) * POOL_GROUP)

    y = _pool_mix(
        lambda g: h[:, lanes(g)],
        lambda g, j: st_ref[POOL_STATE - j, :, lanes(g)],
        lambda w: 1.0 / w,
        pw_ref)
    x1 = x + (y + pb_ref[...]) * ps_ref[...]
    x2 = _ffn(x1, nf0_ref[...], wgu_ref, wd_ref)
    xs2_ref[...] = x2
    hk = _rmsnorm(x2, nkv_ref[...]).astype(BF16)
    k_ref[...] = _dot(hk, wk_ref[...])
    v_ref[...] = _dot(hk, wv_ref[...])
    lf_ref[...] = _log_sigmoid(_dot(hk, wf_ref[...]) + bf_ref[...])
    q_ref[...] = _dot(_rmsnorm(x2, nm1_ref[...]).astype(BF16), wq_ref[...])


PAGE_ROWS = PAGE_SIZE * N_HEADS


def _attn_sample_kernel(pt_ref, q_ref, kn_ref, vn_ref, lfn_ref, expand_ref, *rest, npg):
    del pt_ref
    k_refs = rest[:npg]
    v_refs = rest[npg:2 * npg]
    lf_refs = rest[2 * npg:3 * npg]
    o_ref = rest[3 * npg]
    m_ref, l_ref, acc_ref, carry_ref, pad_ref = rest[3 * npg + 1:]
    j = pl.program_id(1)

    @pl.when(j == 0)
    def _():
        m_ref[...] = jnp.full(m_ref.shape, MASKED, F32)
        l_ref[...] = jnp.zeros(l_ref.shape, F32)
        acc_ref[...] = jnp.zeros(acc_ref.shape, F32)
        carry_ref[...] = jnp.zeros(carry_ref.shape, F32)
        pad_ref[...] = jnp.zeros(pad_ref.shape, F32)

    q = q_ref[0].astype(BF16)
    col = lax.broadcasted_iota(jnp.int32, (N_HEADS, PAGE_ROWS), 1)
    own_head = (col & (N_HEADS - 1)) == lax.broadcasted_iota(jnp.int32, (N_HEADS, PAGE_ROWS), 0)

    def process(pages, valid):
        n = len(pages)
        lf_t = []
        for _, _, lf in pages:
            pad_ref[:, 0:N_HEADS] = lf
            lf_t.append(pad_ref[...].T[0:N_HEADS, :])
        before, total = [], carry_ref[...]
        for t in lf_t:
            before.append(total)
            total = total + jnp.sum(t, axis=1, keepdims=True)
        carry_ref[...] = total
        pieces = jnp.concatenate(_split_bf16(jnp.concatenate(lf_t, axis=0)), axis=0)
        inside = _dot(pieces, expand_ref[...])
        inside = sum(inside[s * n * N_HEADS:(s + 1) * n * N_HEADS] for s in range(N_SPLIT))
        logits = []
        for idx, (k3, _, _) in enumerate(pages):
            k2 = k3.reshape(PAGE_ROWS, HEAD_DIM).astype(BF16)
            s = lax.dot_general(q, k2, (((1,), (1,)), ((), ())), preferred_element_type=F32)
            u = s - inside[idx * N_HEADS:(idx + 1) * N_HEADS] - before[idx]
            logits.append(jnp.where(valid, u, MASKED))
        u = jnp.concatenate(logits, axis=1) if n > 1 else logits[0]
        m_old = m_ref[...]
        m_new = jnp.maximum(m_old, jnp.max(u, axis=1, keepdims=True))
        p = jnp.exp(u - m_new)
        alpha = jnp.exp(m_old - m_new)
        l_ref[...] = l_ref[...] * alpha + jnp.sum(p, axis=1, keepdims=True)
        pv = None
        for idx, (_, v3, _) in enumerate(pages):
            v2 = v3.reshape(PAGE_ROWS, HEAD_DIM).astype(BF16)
            t = _dot(p[:, idx * PAGE_ROWS:(idx + 1) * PAGE_ROWS].astype(BF16), v2)
            pv = t if pv is None else pv + t
        acc_ref[...] = acc_ref[...] * alpha + pv
        m_ref[...] = m_new

    process([(k_refs[n][0], v_refs[n][0], lf_refs[n][0]) for n in range(npg)], own_head)

    @pl.when(j == pl.num_programs(1) - 1)
    def _():
        first3 = lax.broadcasted_iota(jnp.int32, (PAGE_SIZE, N_HEADS, HEAD_DIM), 0) == 0
        first2 = lax.broadcasted_iota(jnp.int32, (PAGE_SIZE, N_HEADS), 0) == 0
        page = (jnp.where(first3, jnp.broadcast_to(kn_ref[...], first3.shape), 0.0),
                jnp.where(first3, jnp.broadcast_to(vn_ref[...], first3.shape), 0.0),
                jnp.where(first2, jnp.broadcast_to(lfn_ref[0], first2.shape), 0.0))
        process([page], own_head & (col < N_HEADS))
        o_ref[0] = acc_ref[...] / l_ref[...]


def _const_spec(shape):
    nd = len(shape)
    return pl.BlockSpec(shape, lambda *_: (0,) * nd, pipeline_mode=pl.Buffered(1))


def _params(*sem):
    return pltpu.CompilerParams(dimension_semantics=sem, vmem_limit_bytes=VMEM_LIMIT_BYTES)


def _row(a):
    return a.reshape(1, -1).astype(F32)


def _placement():
    src = jnp.arange(LANES)[:, None]
    dst = jnp.arange(N_HEADS * LANES)[None, :]
    head, off = dst // LANES, dst % LANES - HEAD_DIM
    hit = (off >= 0) & (off < N_SPLIT) & (src == off * N_HEADS + head)
    return hit.astype(BF16)


def _layer0_prompt(x, nm, nf, pb, ps, pw, wgu, wd, tile):
    b, s, d = x.shape
    tok = pl.BlockSpec((1, tile, d), lambda bi, i: (bi, i, 0))
    return pl.pallas_call(
        functools.partial(_layer0_prompt_kernel, tile=tile),
        grid=(b, s // tile),
        in_specs=[tok] + [_const_spec(a.shape) for a in (nm, nf, pb, ps, pw, wgu, wd)],
        out_specs=[tok, pl.BlockSpec((1, 1, POOL_STATE, d), lambda bi, i: (bi, 0, 0, 0))],
        out_shape=[jax.ShapeDtypeStruct((b, s, d), F32),
                   jax.ShapeDtypeStruct((b, 1, POOL_STATE, d), F32)],
        scratch_shapes=[pltpu.VMEM((HIST + tile, d), F32)],
        compiler_params=_params("arbitrary", "arbitrary"),
        name="layer0_prompt",
    )(x, nm, nf, pb, ps, pw, wgu, wd)


def _kvq_prompt(x2, nkv, nm1, bf, wk, wv, wf, wq, tile):
    b, s, d = x2.shape
    place = _placement()
    tok = pl.BlockSpec((1, tile, d), lambda bi, i: (bi, i, 0))
    consts = (nkv, nm1, bf, wk, wv, wf, wq, place)
    return pl.pallas_call(
        functools.partial(_kvq_prompt_kernel, tile=tile),
        grid=(b, s // tile),
        in_specs=[tok] + [_const_spec(a.shape) for a in consts],
        out_specs=[tok, tok, pl.BlockSpec((1, tile, N_HEADS), lambda bi, i: (bi, i, 0)),
                   pl.BlockSpec((1, tile, N_HEADS * LANES), lambda bi, i: (bi, i, 0)),
                   pl.BlockSpec((1, N_HEADS * Q_ROWS, tile), lambda bi, i: (bi, 0, i)),
                   pl.BlockSpec((1, 1, N_HEADS * V_ROWS, tile), lambda bi, i: (bi, i, 0, 0))],
        out_shape=[jax.ShapeDtypeStruct((b, s, d), F32), jax.ShapeDtypeStruct((b, s, d), F32),
                   jax.ShapeDtypeStruct((b, s, N_HEADS), F32),
                   jax.ShapeDtypeStruct((b, s, N_HEADS * LANES), BF16),
                   jax.ShapeDtypeStruct((b, N_HEADS * Q_ROWS, s), BF16),
                   jax.ShapeDtypeStruct((b, s // tile, N_HEADS * V_ROWS, tile), BF16)],
        scratch_shapes=[pltpu.VMEM((SUBLANES, LANES), F32), pltpu.VMEM((tile, d), F32)],
        compiler_params=_params("arbitrary", "arbitrary"),
        name="kvq_prompt",
    )(x2, *consts)


def _attn_prompt(qt, ks, vt, tq, tk):
    b, s, _ = ks.shape
    assert vt.shape[1] * tk == s and vt.shape[3] == tk
    return pl.pallas_call(
        functools.partial(_attn_prompt_kernel, tq=tq, tk=tk),
        grid=(b, N_HEADS // 2, s // tq),
        in_specs=[pl.BlockSpec((1, 2 * Q_ROWS, tq), lambda bi, hg, i: (bi, hg, i)),
                  pl.BlockSpec((1, s, 2 * LANES), lambda bi, hg, i: (bi, 0, hg)),
                  pl.BlockSpec((1, s // tk, 2 * V_ROWS, tk), lambda bi, hg, i: (bi, 0, hg, 0))],
        out_specs=pl.BlockSpec((1, tq, LANES), lambda bi, hg, i: (bi, i, hg)),
        out_shape=jax.ShapeDtypeStruct((b, s, D_MODEL), BF16),
        scratch_shapes=[pltpu.VMEM((2, 1, tq), F32), pltpu.VMEM((2, V_ROWS, tq), F32),
                        pltpu.VMEM((2 * tq // tk, tk, tq), F32), pltpu.VMEM((2 * tq // tk, tk, tq), BF16),
                        pltpu.VMEM((2 * tq // tk, 1, tq), F32)],
        compiler_params=_params("parallel", "parallel", "arbitrary"),
        name="attn_prompt",
    )(qt, ks, vt)


def _post(o, x2, wo, nf, wgu, wd, nfin, tile):
    b, s, d = x2.shape
    tok = pl.BlockSpec((1, tile, d), lambda bi, i: (bi, i, 0))
    consts = (wo, nf, wgu, wd, nfin)
    return pl.pallas_call(
        _post_kernel,
        grid=(b, s // tile),
        in_specs=[tok, tok] + [_const_spec(a.shape) for a in consts],
        out_specs=tok,
        out_shape=jax.ShapeDtypeStruct((b, s, d), F32),
        compiler_params=_params("parallel", "parallel"),
        name="post",
    )(o, x2, *consts)


def _pre_sample(xs, st, consts):
    n, d = xs.shape
    out_shape = [jax.ShapeDtypeStruct((n, d), F32), jax.ShapeDtypeStruct((POOL_STATE, n, d), F32),
                 jax.ShapeDtypeStruct((n, d), F32), jax.ShapeDtypeStruct((n, d), F32),
                 jax.ShapeDtypeStruct((n, LANES), F32), jax.ShapeDtypeStruct((n, d), F32)]
    return pl.pallas_call(
        _pre_sample_kernel,
        grid=(1,),
        in_specs=[_const_spec(a.shape) for a in (xs, st) + consts],
        out_specs=[pl.BlockSpec(o.shape, lambda i, nd=len(o.shape): (0,) * nd) for o in out_shape],
        out_shape=out_shape,
        compiler_params=_params("arbitrary"),
        name="pre_sample",
    )(xs, st, *consts)


def _prefix_expansion():
    src = jnp.arange(PAGE_SIZE)[:, None]
    dst = jnp.arange(PAGE_ROWS)[None, :] // N_HEADS
    return (src <= dst).astype(BF16)


def _attn_sample(page_table, q, kn, vn, lfn, cache_k, cache_v, cache_lf, npg):
    n, n_pages = page_table.shape
    expand = _prefix_expansion()
    per_token = pl.BlockSpec((1, N_HEADS, HEAD_DIM), lambda bi, j, pt: (bi, 0, 0))

    def paged(*minor):
        def spec(p):
            return pl.BlockSpec((1, PAGE_SIZE) + minor, lambda bi, j, pt: (pt[bi, j * npg + p],) + (0,) * (len(minor) + 1))
        return [spec(p) for p in range(npg)]

    in_specs = ([per_token, per_token, per_token, pl.BlockSpec((1, 1, N_HEADS), lambda bi, j, pt: (bi, 0, 0)),
                 pl.BlockSpec(expand.shape, lambda bi, j, pt: (0, 0), pipeline_mode=pl.Buffered(1))]
                + paged(N_HEADS, HEAD_DIM) + paged(N_HEADS, HEAD_DIM) + paged(N_HEADS))
    return pl.pallas_call(
        functools.partial(_attn_sample_kernel, npg=npg),
        grid_spec=pltpu.PrefetchScalarGridSpec(
            num_scalar_prefetch=1,
            grid=(n, n_pages // npg),
            in_specs=in_specs,
            out_specs=per_token,
            scratch_shapes=[pltpu.VMEM((N_HEADS, 1), F32), pltpu.VMEM((N_HEADS, 1), F32),
                            pltpu.VMEM((N_HEADS, HEAD_DIM), F32), pltpu.VMEM((N_HEADS, 1), F32),
                            pltpu.VMEM((PAGE_SIZE, LANES), F32)]),
        out_shape=jax.ShapeDtypeStruct((n, N_HEADS, HEAD_DIM), F32),
        compiler_params=_params("parallel", "arbitrary"),
        name="attn_sample",
    )(page_table, q, kn, vn, lfn, expand, *([cache_k] * npg), *([cache_v] * npg), *([cache_lf] * npg))


def kernel(x_prompt, x_sample, state_pool, cache_k, cache_v, cache_logf, page_table, norm_mix, norm_ffn,
           pool_w, pool_b, pool_scale, norm_kv, w_kvf, b_f, w_q, w_o, w_gate_up, w_down, norm_final):
    bp, s, d = x_prompt.shape
    bd = x_sample.shape[0]

    pw = pool_w[0].astype(BF16)
    wgu = w_gate_up.astype(BF16)
    wd = w_down.astype(BF16)
    wk = w_kvf[:, :d].astype(BF16)
    wv = w_kvf[:, d:2 * d].astype(BF16)
    wf = jnp.pad(w_kvf[:, 2 * d:], ((0, 0), (0, LANES - N_HEADS))).astype(BF16)
    bf = jnp.pad(b_f, (0, LANES - N_HEADS)).reshape(1, LANES).astype(F32)
    wq = (w_q[0] * ATTN_SCALE).astype(BF16)
    wo = w_o[0].astype(BF16)
    nm0, nm1 = _row(norm_mix[0]), _row(norm_mix[1])
    nf0, nf1 = _row(norm_ffn[0]), _row(norm_ffn[1])
    pb, ps = _row(pool_b[0]), _row(pool_scale[0])
    nkv, nfin = _row(norm_kv), _row(norm_final)

    x2p, pool_state_prompt = _layer0_prompt(x_prompt, nm0, nf0, pb, ps, pw, wgu[0], wd[0], TILE_TOKENS)
    k_p, v_p, lf_p, ks, qt, vt = _kvq_prompt(x2p, nkv, nm1, bf, wk, wv, wf, wq, ATTN_TK)
    o_p = _attn_prompt(qt, ks, vt, ATTN_TQ, ATTN_TK)
    y_prompt = _post(o_p, x2p, wo, nf1, wgu[1], wd[1], nfin, TILE_TOKENS)

    xs = x_sample.reshape(bd, d)
    st = jnp.transpose(state_pool[:, 0], (1, 0, 2))
    pre_consts = (nm0, nf0, pb, ps, pw, wgu[0], wd[0], nkv, nm1, bf, wk, wv, wf, wq)
    xs2, pst, k_s, v_s, lf_s, q_s = _pre_sample(xs, st, pre_consts)
    by_head = lambda a: a.reshape(bd, N_HEADS, HEAD_DIM)
    o_s = _attn_sample(page_table, by_head(q_s), by_head(k_s), by_head(v_s), lf_s[:, :N_HEADS].reshape(bd, 1, N_HEADS),
                       cache_k, cache_v, cache_logf, PAGES_PER_STEP)
    y_sample = _post(o_s.reshape(1, bd, d), xs2.reshape(1, bd, d), wo, nf1, wgu[1], wd[1], nfin, bd)

    heads = lambda a, n: a.reshape(a.shape[0], n, N_HEADS, HEAD_DIM)
    return (y_prompt, y_sample.reshape(bd, 1, d),
            pool_state_prompt, jnp.transpose(pst, (1, 0, 2)).reshape(bd, 1, POOL_STATE, d),
            heads(k_p, s), heads(v_p, s), lf_p,
            heads(k_s, 1), heads(v_s, 1), lf_s[:, :N_HEADS].reshape(bd, 1, N_HEADS))
```

```python
import functools

import jax
import jax.numpy as jnp
from jax import lax
from jax.experimental import pallas as pl
from jax.experimental.pallas import tpu as pltpu

D_MODEL = 1024
N_HEADS = 16
HEAD_DIM = 64
D_FF = 2816
POOL_WINDOWS = (2, 4, 8, 16)
POOL_GROUP = D_MODEL // len(POOL_WINDOWS)
POOL_STATE = max(POOL_WINDOWS) - 1
EPS = 1e-6
ATTN_SCALE = HEAD_DIM ** -0.5
PAGE_SIZE = 128

LANES = 128
SUBLANES = 8
HIST = 2 * SUBLANES
N_SPLIT = 3
MASKED = -1e30
LOG2E = 1.4426950408889634
Q_ROWS = LANES
FILL_ROWS = 16
V_ROWS = HEAD_DIM + FILL_ROWS
VMEM_LIMIT_BYTES = 56 * 1024 * 1024

TILE_TOKENS = 256
ATTN_TQ = 512
ATTN_TK = 256
PAGES_PER_STEP = 8

F32 = jnp.float32
BF16 = jnp.bfloat16


def _dot(a, b):
    return jnp.dot(a, b, preferred_element_type=F32)


def _rmsnorm(x, g):
    ms = jnp.mean(x * x, axis=-1, keepdims=True)
    return x * lax.rsqrt(ms + EPS) * g


def _log_sigmoid(x):
    return jnp.minimum(x, 0.0) - jnp.log1p(jnp.exp(-jnp.abs(x)))


def _ffn(x, g_norm, wgu_ref, wd_ref):
    h = _rmsnorm(x, g_norm).astype(BF16)
    gu = _dot(h, wgu_ref[...])
    g = gu[:, :D_FF]
    u = gu[:, D_FF:]
    a = g * jax.nn.sigmoid(g) * u
    return x + _dot(a.astype(BF16), wd_ref[...])


def _split_bf16(x):
    pieces = []
    r = x
    for _ in range(N_SPLIT):
        p = r.astype(BF16)
        pieces.append(p)
        r = r - p.astype(F32)
    return pieces


def _lower_tri(n):
    r = lax.broadcasted_iota(jnp.int32, (n, n), 0)
    c = lax.broadcasted_iota(jnp.int32, (n, n), 1)
    return jnp.where(r >= c, 1.0, 0.0).astype(BF16)


def _prefix_sum_rows(x, tri):
    out = None
    for p in _split_bf16(x):
        t = _dot(tri, p)
        out = t if out is None else out + t
    return out


def _pool_mix(cur_fn, shifted_fn, inv_cnt_fn, pw_ref):
    outs = []
    for g, w in enumerate(POOL_WINDOWS):
        cur = cur_fn(g)
        s = cur
        for j in range(1, w):
            s = s + shifted_fn(g, j)
        d = s * inv_cnt_fn(w) - cur
        outs.append(_dot(d.astype(BF16), pw_ref[g]))
    return jnp.concatenate(outs, axis=-1)


def _layer0_prompt_kernel(x_ref, nm_ref, nf_ref, pb_ref, ps_ref, pw_ref, wgu_ref, wd_ref,
                          x2_ref, pstate_ref, hext_ref, *, tile):
    i = pl.program_id(1)

    @pl.when(i == 0)
    def _():
        hext_ref[0:HIST, :] = jnp.zeros((HIST, D_MODEL), F32)

    x = x_ref[0]
    h = _rmsnorm(x, nm_ref[...])
    hext_ref[HIST:HIST + tile, :] = h
    pos = lax.broadcasted_iota(jnp.int32, (tile, 1), 0) + i * tile

    def lanes(g):
        return slice(g * POOL_GROUP, (g + 1) * POOL_GROUP)

    y = _pool_mix(
        lambda g: h[:, lanes(g)],
        lambda g, j: hext_ref[HIST - j:HIST - j + tile, lanes(g)],
        lambda w: 1.0 / jnp.minimum(pos + 1, w).astype(F32),
        pw_ref)
    x1 = x + (y + pb_ref[...]) * ps_ref[...]
    x2_ref[0] = _ffn(x1, nf_ref[...], wgu_ref, wd_ref)

    hext_ref[0:HIST, :] = hext_ref[tile:tile + HIST, :]

    @pl.when(i == pl.num_programs(1) - 1)
    def _():
        pstate_ref[0, 0] = hext_ref[HIST - POOL_STATE:HIST, :]


def _kvq_prompt_kernel(x2_ref, nkv_ref, nm1_ref, bf_ref, wk_ref, wv_ref, wf_ref, wq_ref, place_ref,
                       k_ref, v_ref, lf_ref, ks_ref, qt_ref, vt_ref, carry_ref, q_ref, *, tile):
    i = pl.program_id(1)

    @pl.when(i == 0)
    def _():
        carry_ref[...] = jnp.zeros(carry_ref.shape, F32)

    x2 = x2_ref[0]
    hk = _rmsnorm(x2, nkv_ref[...]).astype(BF16)
    k = _dot(hk, wk_ref[...])
    v = _dot(hk, wv_ref[...])
    lf = _log_sigmoid(_dot(hk, wf_ref[...]) + bf_ref[...])
    k_ref[0] = k
    v_ref[0] = v
    lf_ref[0] = lf[:, :N_HEADS]

    lane = lax.broadcasted_iota(jnp.int32, (tile, LANES), 1)
    lf = jnp.where(lane < N_HEADS, lf, 0.0)
    fcum = _prefix_sum_rows(lf, _lower_tri(tile)) + carry_ref[0:1, :]
    carry_ref[0:1, :] = fcum[tile - 1:tile, :]

    packed = None
    for n, p in enumerate(_split_bf16(-LOG2E * fcum)):
        p = p.astype(F32)
        p = p if n == 0 else pltpu.roll(p, n * N_HEADS, 1)
        packed = p if packed is None else packed + p
    bias = _dot(packed.astype(BF16), place_ref[...])

    is_data = lane < HEAD_DIM
    for pair in range(N_HEADS // 2):
        cols = k[:, pair * LANES:(pair + 1) * LANES]
        for odd in range(2):
            slab = slice((2 * pair + odd) * LANES, (2 * pair + odd + 1) * LANES)
            data = pltpu.roll(cols, HEAD_DIM, 1) if odd else cols
            ks_ref[0, :, slab] = jnp.where(is_data, data, bias[:, slab]).astype(BF16)

    q_ref[...] = LOG2E * _dot(_rmsnorm(x2, nm1_ref[...]).astype(BF16), wq_ref[...])
    qt = q_ref[...].T
    vt = v_ref[0].T
    fill_row = lax.broadcasted_iota(jnp.int32, (FILL_ROWS, tile), 0)
    q_fill = jnp.where(fill_row < N_SPLIT, 1.0, 0.0).astype(BF16)
    v_fill = jnp.where(fill_row == 0, 1.0, 0.0).astype(BF16)
    q_zero = jnp.zeros((Q_ROWS - HEAD_DIM - FILL_ROWS, tile), BF16)
    for h in range(N_HEADS):
        rows = slice(h * HEAD_DIM, (h + 1) * HEAD_DIM)
        q0, v0 = h * Q_ROWS, h * V_ROWS
        qt_ref[0, q0:q0 + HEAD_DIM, :] = qt[rows, :].astype(BF16)
        qt_ref[0, q0 + HEAD_DIM:q0 + HEAD_DIM + FILL_ROWS, :] = q_fill
        qt_ref[0, q0 + HEAD_DIM + FILL_ROWS:q0 + Q_ROWS, :] = q_zero
        vt_ref[0, 0, v0:v0 + HEAD_DIM, :] = vt[rows, :].astype(BF16)
        vt_ref[0, 0, v0 + HEAD_DIM:v0 + V_ROWS, :] = v_fill


def _attn_prompt_kernel(qt_ref, ks_ref, vt_ref, o_ref, m_ref, acc_ref, s_ref, p_ref, a_ref, *, tq, tk):
    i = pl.program_id(2)
    heads = range(2)
    per_q = tq // tk
    units = [(d, hh) for d in range(per_q) for hh in heads]
    qts = [qt_ref[0, hh * Q_ROWS:(hh + 1) * Q_ROWS, :] for hh in heads]

    def scores(kblk, hh):
        start = pl.multiple_of(kblk * tk, tk)
        return _dot(ks_ref[0, pl.ds(start, tk), hh * LANES:(hh + 1) * LANES], qts[hh])

    def softmax(s, hh):
        m_old = m_ref[hh]
        m_new = jnp.maximum(m_old, jnp.max(s, axis=0, keepdims=True))
        m_ref[hh] = m_new
        return jnp.exp2(s - m_new).astype(BF16), jnp.exp2(m_old - m_new)

    def accumulate(p, alpha, kblk, hh):
        vb = vt_ref[0, kblk, hh * V_ROWS:(hh + 1) * V_ROWS, :]
        acc_ref[hh] = acc_ref[hh] * alpha + _dot(vb, p)

    for hh in heads:
        m_ref[hh] = jnp.full((1, tq), MASKED, F32)
        acc_ref[hh] = jnp.zeros((V_ROWS, tq), F32)

    def value_stage(jj):
        for u, (d, hh) in enumerate(units):
            accumulate(p_ref[u], a_ref[u], jj * per_q + d, hh)

    def softmax_stage(diagonal):
        for u, (d, hh) in enumerate(units):
            s = s_ref[u]
            if diagonal:
                key = lax.broadcasted_iota(jnp.int32, s.shape, 0) + d * tk
                qry = lax.broadcasted_iota(jnp.int32, s.shape, 1)
                s = jnp.where(key <= qry, s, MASKED)
            p_ref[u], a_ref[u] = softmax(s, hh)

    def score_stage(jj):
        for u, (d, hh) in enumerate(units):
            s_ref[u] = scores(jj * per_q + d, hh)

    for u in range(len(units)):
        p_ref[u] = jnp.zeros((tk, tq), BF16)
        a_ref[u] = jnp.ones((1, tq), F32)
    score_stage(0)

    def body(jj, carry):
        value_stage(jnp.maximum(jj - 1, 0))
        softmax_stage(diagonal=False)
        score_stage(jj + 1)
        return carry

    lax.fori_loop(0, i, body, 0)
    value_stage(jnp.maximum(i - 1, 0))
    softmax_stage(diagonal=True)
    value_stage(i)

    outs = []
    for hh in heads:
        acc = acc_ref[hh]
        outs.append(acc[0:HEAD_DIM, :] / acc[HEAD_DIM:HEAD_DIM + 1, :])
    o_ref[0] = jnp.concatenate(outs, axis=0).T.astype(BF16)


def _post_kernel(o_ref, x2_ref, wo_ref, nf_ref, wgu_ref, wd_ref, nfin_ref, y_ref):
    x3 = x2_ref[0] + _dot(o_ref[0].astype(BF16), wo_ref[...])
    x4 = _ffn(x3, nf_ref[...], wgu_ref, wd_ref)
    y_ref[0] = _rmsnorm(x4, nfin_ref[...])


def _pre_sample_kernel(xs_ref, st_ref, nm0_ref, nf0_ref, pb_ref, ps_ref, pw_ref, wgu_ref, wd_ref,
                       nkv_ref, nm1_ref, bf_ref, wk_ref, wv_ref, wf_ref, wq_ref,
                       xs2_ref, pst_ref, k_ref, v_ref, lf_ref, q_ref):
    x = xs_ref[...]
    h = _rmsnorm(x, nm0_ref[...])
    for r in range(POOL_STATE - 1):
        pst_ref[r] = st_ref[r + 1]
    pst_ref[POOL_STATE - 1] = h

    def lanes(g):
        return slice(g * POOL_GROUP, (g + 1) * POOL_GROUP)

    y = _pool_mix(
        lambda g: h[:, lanes(g)],
        lambda g, j: st_ref[POOL_STATE - j, :, lanes(g)],
        lambda w: 1.0 / w,
        pw_ref)
    x1 = x + (y + pb_ref[...]) * ps_ref[...]
    x2 = _ffn(x1, nf0_ref[...], wgu_ref, wd_ref)
    xs2_ref[...] = x2
    hk = _rmsnorm(x2, nkv_ref[...]).astype(BF16)
    k_ref[...] = _dot(hk, wk_ref[...])
    v_ref[...] = _dot(hk, wv_ref[...])
    lf_ref[...] = _log_sigmoid(_dot(hk, wf_ref[...]) + bf_ref[...])
    q_ref[...] = _dot(_rmsnorm(x2, nm1_ref[...]).astype(BF16), wq_ref[...])


def _attn_sample_kernel(pt_ref, qb_ref, kn_ref, vn_ref, lfn_ref, tri_ref, *rest, npg):
    del pt_ref
    k_refs = rest[:npg]
    v_refs = rest[npg:2 * npg]
    lf_refs = rest[2 * npg:3 * npg]
    o_ref = rest[3 * npg]
    m_ref, l_ref, acc_ref, carry_ref = rest[3 * npg + 1:]
    j = pl.program_id(1)

    @pl.when(j == 0)
    def _():
        m_ref[...] = jnp.full(m_ref.shape, MASKED, F32)
        l_ref[...] = jnp.zeros(l_ref.shape, F32)
        acc_ref[...] = jnp.zeros(acc_ref.shape, F32)
        carry_ref[...] = jnp.zeros(carry_ref.shape, F32)

    def process(pages, valid):
        n = len(pages)
        lf = jnp.concatenate([r[0] for _, _, r in pages], axis=0) if n > 1 else pages[0][2][0]
        inside = None
        for piece in _split_bf16(lf):
            t = _dot(piece, tri_ref[...])
            inside = t if inside is None else inside + t
        total = carry_ref[...]
        logits = []
        for idx, (k_ref, _, _) in enumerate(pages):
            rows = slice(idx * N_HEADS, (idx + 1) * N_HEADS)
            s = jnp.sum(k_ref[0] * qb_ref[0], axis=1)
            u = s - inside[rows] - total
            logits.append(u if valid is None else jnp.where(valid, u, MASKED))
            total = total + inside[rows, PAGE_SIZE - 1:PAGE_SIZE]
        carry_ref[...] = total
        u = jnp.concatenate(logits, axis=1) if n > 1 else logits[0]
        m_old = m_ref[...]
        m_new = jnp.maximum(m_old, jnp.max(u, axis=1, keepdims=True))
        p = jnp.exp(u - m_new)
        alpha = jnp.exp(m_old - m_new)
        l_ref[...] = l_ref[...] * alpha + jnp.sum(p, axis=1, keepdims=True)
        m_ref[...] = m_new
        for h in range(N_HEADS):
            a = acc_ref[h] * alpha[h:h + 1, :]
            for idx, (_, v_ref, _) in enumerate(pages):
                a = a + v_ref[0, h] * p[h:h + 1, idx * PAGE_SIZE:(idx + 1) * PAGE_SIZE]
            acc_ref[h] = a

    process([(k_refs[n], v_refs[n], lf_refs[n]) for n in range(npg)], None)

    @pl.when(j == pl.num_programs(1) - 1)
    def _():
        pos = lax.broadcasted_iota(jnp.int32, (N_HEADS, PAGE_SIZE), 1)
        process([(kn_ref, vn_ref, lfn_ref)], pos == 0)
        o_ref[0] = jnp.sum(acc_ref[...], axis=2) / l_ref[...]


def _const_spec(shape):
    nd = len(shape)
    return pl.BlockSpec(shape, lambda *_: (0,) * nd, pipeline_mode=pl.Buffered(1))


def _params(*sem):
    return pltpu.CompilerParams(dimension_semantics=sem, vmem_limit_bytes=VMEM_LIMIT_BYTES)


def _row(a):
    return a.reshape(1, -1).astype(F32)


def _placement():
    src = jnp.arange(LANES)[:, None]
    dst = jnp.arange(N_HEADS * LANES)[None, :]
    head, off = dst // LANES, dst % LANES - HEAD_DIM
    hit = (off >= 0) & (off < N_SPLIT) & (src == off * N_HEADS + head)
    return hit.astype(BF16)


def _layer0_prompt(x, nm, nf, pb, ps, pw, wgu, wd, tile):
    b, s, d = x.shape
    tok = pl.BlockSpec((1, tile, d), lambda bi, i: (bi, i, 0))
    return pl.pallas_call(
        functools.partial(_layer0_prompt_kernel, tile=tile),
        grid=(b, s // tile),
        in_specs=[tok] + [_const_spec(a.shape) for a in (nm, nf, pb, ps, pw, wgu, wd)],
        out_specs=[tok, pl.BlockSpec((1, 1, POOL_STATE, d), lambda bi, i: (bi, 0, 0, 0))],
        out_shape=[jax.ShapeDtypeStruct((b, s, d), F32),
                   jax.ShapeDtypeStruct((b, 1, POOL_STATE, d), F32)],
        scratch_shapes=[pltpu.VMEM((HIST + tile, d), F32)],
        compiler_params=_params("arbitrary", "arbitrary"),
        name="layer0_prompt",
    )(x, nm, nf, pb, ps, pw, wgu, wd)


def _kvq_prompt(x2, nkv, nm1, bf, wk, wv, wf, wq, tile):
    b, s, d = x2.shape
    place = _placement()
    tok = pl.BlockSpec((1, tile, d), lambda bi, i: (bi, i, 0))
    consts = (nkv, nm1, bf, wk, wv, wf, wq, place)
    return pl.pallas_call(
        functools.partial(_kvq_prompt_kernel, tile=tile),
        grid=(b, s // tile),
        in_specs=[tok] + [_const_spec(a.shape) for a in consts],
        out_specs=[tok, tok, pl.BlockSpec((1, tile, N_HEADS), lambda bi, i: (bi, i, 0)),
                   pl.BlockSpec((1, tile, N_HEADS * LANES), lambda bi, i: (bi, i, 0)),
                   pl.BlockSpec((1, N_HEADS * Q_ROWS, tile), lambda bi, i: (bi, 0, i)),
                   pl.BlockSpec((1, 1, N_HEADS * V_ROWS, tile), lambda bi, i: (bi, i, 0, 0))],
        out_shape=[jax.ShapeDtypeStruct((b, s, d), F32), jax.ShapeDtypeStruct((b, s, d), F32),
                   jax.ShapeDtypeStruct((b, s, N_HEADS), F32),
                   jax.ShapeDtypeStruct((b, s, N_HEADS * LANES), BF16),
                   jax.ShapeDtypeStruct((b, N_HEADS * Q_ROWS, s), BF16),
                   jax.ShapeDtypeStruct((b, s // tile, N_HEADS * V_ROWS, tile), BF16)],
        scratch_shapes=[pltpu.VMEM((SUBLANES, LANES), F32), pltpu.VMEM((tile, d), F32)],
        compiler_params=_params("arbitrary", "arbitrary"),
        name="kvq_prompt",
    )(x2, *consts)


def _attn_prompt(qt, ks, vt, tq, tk):
    b, s, _ = ks.shape
    assert vt.shape[1] * tk == s and vt.shape[3] == tk
    return pl.pallas_call(
        functools.partial(_attn_prompt_kernel, tq=tq, tk=tk),
        grid=(b, N_HEADS // 2, s // tq),
        in_specs=[pl.BlockSpec((1, 2 * Q_ROWS, tq), lambda bi, hg, i: (bi, hg, i)),
                  pl.BlockSpec((1, s, 2 * LANES), lambda bi, hg, i: (bi, 0, hg)),
                  pl.BlockSpec((1, s // tk, 2 * V_ROWS, tk), lambda bi, hg, i: (bi, 0, hg, 0))],
        out_specs=pl.BlockSpec((1, tq, LANES), lambda bi, hg, i: (bi, i, hg)),
        out_shape=jax.ShapeDtypeStruct((b, s, D_MODEL), BF16),
        scratch_shapes=[pltpu.VMEM((2, 1, tq), F32), pltpu.VMEM((2, V_ROWS, tq), F32),
                        pltpu.VMEM((2 * tq // tk, tk, tq), F32), pltpu.VMEM((2 * tq // tk, tk, tq), BF16),
                        pltpu.VMEM((2 * tq // tk, 1, tq), F32)],
        compiler_params=_params("parallel", "parallel", "arbitrary"),
        name="attn_prompt",
    )(qt, ks, vt)


def _post(o, x2, wo, nf, wgu, wd, nfin, tile):
    b, s, d = x2.shape
    tok = pl.BlockSpec((1, tile, d), lambda bi, i: (bi, i, 0))
    consts = (wo, nf, wgu, wd, nfin)
    return pl.pallas_call(
        _post_kernel,
        grid=(b, s // tile),
        in_specs=[tok, tok] + [_const_spec(a.shape) for a in consts],
        out_specs=tok,
        out_shape=jax.ShapeDtypeStruct((b, s, d), F32),
        compiler_params=_params("parallel", "parallel"),
        name="post",
    )(o, x2, *consts)


def _pre_sample(xs, st, consts):
    n, d = xs.shape
    out_shape = [jax.ShapeDtypeStruct((n, d), F32), jax.ShapeDtypeStruct((POOL_STATE, n, d), F32),
                 jax.ShapeDtypeStruct((n, d), F32), jax.ShapeDtypeStruct((n, d), F32),
                 jax.ShapeDtypeStruct((n, LANES), F32), jax.ShapeDtypeStruct((n, d), F32)]
    return pl.pallas_call(
        _pre_sample_kernel,
        grid=(1,),
        in_specs=[_const_spec(a.shape) for a in (xs, st) + consts],
        out_specs=[pl.BlockSpec(o.shape, lambda i, nd=len(o.shape): (0,) * nd) for o in out_shape],
        out_shape=out_shape,
        compiler_params=_params("arbitrary"),
        name="pre_sample",
    )(xs, st, *consts)


def _attn_sample(page_table, q, k_new, v_new, lf_new, cache_k, cache_v, cache_lf, npg):
    n, n_pages = page_table.shape
    cache_k = jnp.transpose(cache_k, (0, 2, 3, 1))
    cache_v = jnp.transpose(cache_v, (0, 2, 3, 1))
    cache_lf = jnp.transpose(cache_lf, (0, 2, 1))
    qb = jnp.broadcast_to(q[..., None], q.shape + (PAGE_SIZE,))
    first = lambda a: jnp.pad(a[..., None], ((0, 0),) * a.ndim + ((0, PAGE_SIZE - 1),))
    pos = jnp.arange(PAGE_SIZE)
    tri = (pos[:, None] <= pos[None, :]).astype(BF16)

    kv_token = pl.BlockSpec((1, N_HEADS, HEAD_DIM, PAGE_SIZE), lambda bi, j, pt: (bi, 0, 0, 0))
    lf_token = pl.BlockSpec((1, N_HEADS, PAGE_SIZE), lambda bi, j, pt: (bi, 0, 0))

    def paged(*minor):
        def spec(p):
            return pl.BlockSpec((1,) + minor, lambda bi, j, pt: (pt[bi, j * npg + p],) + (0,) * len(minor))
        return [spec(p) for p in range(npg)]

    in_specs = ([kv_token, kv_token, kv_token, lf_token,
                 pl.BlockSpec(tri.shape, lambda bi, j, pt: (0, 0), pipeline_mode=pl.Buffered(1))]
                + paged(N_HEADS, HEAD_DIM, PAGE_SIZE) + paged(N_HEADS, HEAD_DIM, PAGE_SIZE) + paged(N_HEADS, PAGE_SIZE))
    return pl.pallas_call(
        functools.partial(_attn_sample_kernel, npg=npg),
        grid_spec=pltpu.PrefetchScalarGridSpec(
            num_scalar_prefetch=1,
            grid=(n, n_pages // npg),
            in_specs=in_specs,
            out_specs=pl.BlockSpec((1, N_HEADS, HEAD_DIM), lambda bi, j, pt: (bi, 0, 0)),
            scratch_shapes=[pltpu.VMEM((N_HEADS, 1), F32), pltpu.VMEM((N_HEADS, 1), F32),
                            pltpu.VMEM((N_HEADS, HEAD_DIM, PAGE_SIZE), F32), pltpu.VMEM((N_HEADS, 1), F32)]),
        out_shape=jax.ShapeDtypeStruct((n, N_HEADS, HEAD_DIM), F32),
        compiler_params=_params("parallel", "arbitrary"),
        name="attn_sample",
    )(page_table, qb, first(k_new), first(v_new), first(lf_new), tri,
      *([cache_k] * npg), *([cache_v] * npg), *([cache_lf] * npg))


def kernel(x_prompt, x_sample, state_pool, cache_k, cache_v, cache_logf, page_table, norm_mix, norm_ffn,
           pool_w, pool_b, pool_scale, norm_kv, w_kvf, b_f, w_q, w_o, w_gate_up, w_down, norm_final):
    bp, s, d = x_prompt.shape
    bd = x_sample.shape[0]

    pw = pool_w[0].astype(BF16)
    wgu = w_gate_up.astype(BF16)
    wd = w_down.astype(BF16)
    wk = w_kvf[:, :d].astype(BF16)
    wv = w_kvf[:, d:2 * d].astype(BF16)
    wf = jnp.pad(w_kvf[:, 2 * d:], ((0, 0), (0, LANES - N_HEADS))).astype(BF16)
    bf = jnp.pad(b_f, (0, LANES - N_HEADS)).reshape(1, LANES).astype(F32)
    wq = (w_q[0] * ATTN_SCALE).astype(BF16)
    wo = w_o[0].astype(BF16)
    nm0, nm1 = _row(norm_mix[0]), _row(norm_mix[1])
    nf0, nf1 = _row(norm_ffn[0]), _row(norm_ffn[1])
    pb, ps = _row(pool_b[0]), _row(pool_scale[0])
    nkv, nfin = _row(norm_kv), _row(norm_final)

    x2p, pool_state_prompt = _layer0_prompt(x_prompt, nm0, nf0, pb, ps, pw, wgu[0], wd[0], TILE_TOKENS)
    k_p, v_p, lf_p, ks, qt, vt = _kvq_prompt(x2p, nkv, nm1, bf, wk, wv, wf, wq, ATTN_TK)
    o_p = _attn_prompt(qt, ks, vt, ATTN_TQ, ATTN_TK)
    y_prompt = _post(o_p, x2p, wo, nf1, wgu[1], wd[1], nfin, TILE_TOKENS)

    xs = x_sample.reshape(bd, d)
    st = jnp.transpose(state_pool[:, 0], (1, 0, 2))
    pre_consts = (nm0, nf0, pb, ps, pw, wgu[0], wd[0], nkv, nm1, bf, wk, wv, wf, wq)
    xs2, pst, k_s, v_s, lf_s, q_s = _pre_sample(xs, st, pre_consts)
    by_head = lambda a: a.reshape(bd, N_HEADS, HEAD_DIM)
    o_s = _attn_sample(page_table, by_head(q_s), by_head(k_s), by_head(v_s), lf_s[:, :N_HEADS],
                       cache_k, cache_v, cache_logf, PAGES_PER_STEP)
    y_sample = _post(o_s.reshape(1, bd, d), xs2.reshape(1, bd, d), wo, nf1, wgu[1], wd[1], nfin, bd)

    heads = lambda a, n: a.reshape(a.shape[0], n, N_HEADS, HEAD_DIM)
    return (y_prompt, y_sample.reshape(bd, 1, d),
            pool_state_prompt, jnp.transpose(pst, (1, 0, 2)).reshape(bd, 1, POOL_STATE, d),
            heads(k_p, s), heads(v_p, s), lf_p,
            heads(k_s, 1), heads(v_s, 1), lf_s[:, :N_HEADS].reshape(bd, 1, N_HEADS))
```

```python
import functools

import jax
import jax.numpy as jnp
from jax import lax
from jax.experimental import pallas as pl
from jax.experimental.pallas import tpu as pltpu

D_MODEL = 1024
N_HEADS = 16
HEAD_DIM = 64
D_FF = 2816
POOL_WINDOWS = (2, 4, 8, 16)
POOL_GROUP = D_MODEL // len(POOL_WINDOWS)
POOL_STATE = max(POOL_WINDOWS) - 1
EPS = 1e-6
ATTN_SCALE = HEAD_DIM ** -0.5
PAGE_SIZE = 128

LANES = 128
SUBLANES = 8
HIST = 2 * SUBLANES
N_SPLIT = 3
MASKED = -1e30
LOG2E = 1.4426950408889634
Q_ROWS = LANES
FILL_ROWS = 16
V_ROWS = HEAD_DIM + FILL_ROWS
VMEM_LIMIT_BYTES = 56 * 1024 * 1024

TILE_TOKENS = 256
ATTN_TQ = 512
ATTN_TK = 256
PAGES_PER_STEP = 8

F32 = jnp.float32
BF16 = jnp.bfloat16


def _dot(a, b):
    return jnp.dot(a, b, preferred_element_type=F32)


def _rmsnorm(x, g):
    ms = jnp.mean(x * x, axis=-1, keepdims=True)
    return x * lax.rsqrt(ms + EPS) * g


def _log_sigmoid(x):
    return jnp.minimum(x, 0.0) - jnp.log1p(jnp.exp(-jnp.abs(x)))


def _ffn(x, g_norm, wgu_ref, wd_ref):
    h = _rmsnorm(x, g_norm).astype(BF16)
    gu = _dot(h, wgu_ref[...])
    g = gu[:, :D_FF]
    u = gu[:, D_FF:]
    a = g * jax.nn.sigmoid(g) * u
    return x + _dot(a.astype(BF16), wd_ref[...])


def _split_bf16(x):
    pieces = []
    r = x
    for _ in range(N_SPLIT):
        p = r.astype(BF16)
        pieces.append(p)
        r = r - p.astype(F32)
    return pieces


def _lower_tri(n):
    r = lax.broadcasted_iota(jnp.int32, (n, n), 0)
    c = lax.broadcasted_iota(jnp.int32, (n, n), 1)
    return jnp.where(r >= c, 1.0, 0.0).astype(BF16)


def _prefix_sum_rows(x, tri):
    out = None
    for p in _split_bf16(x):
        t = _dot(tri, p)
        out = t if out is None else out + t
    return out


def _pool_mix(cur_fn, shifted_fn, inv_cnt_fn, pw_ref):
    outs = []
    for g, w in enumerate(POOL_WINDOWS):
        cur = cur_fn(g)
        s = cur
        for j in range(1, w):
            s = s + shifted_fn(g, j)
        d = s * inv_cnt_fn(w) - cur
        outs.append(_dot(d.astype(BF16), pw_ref[g]))
    return jnp.concatenate(outs, axis=-1)


def _layer0_prompt_kernel(x_ref, nm_ref, nf_ref, pb_ref, ps_ref, pw_ref, wgu_ref, wd_ref,
                          x2_ref, pstate_ref, hext_ref, *, tile):
    i = pl.program_id(1)

    @pl.when(i == 0)
    def _():
        hext_ref[0:HIST, :] = jnp.zeros((HIST, D_MODEL), F32)

    x = x_ref[0]
    h = _rmsnorm(x, nm_ref[...])
    hext_ref[HIST:HIST + tile, :] = h
    pos = lax.broadcasted_iota(jnp.int32, (tile, 1), 0) + i * tile

    def lanes(g):
        return slice(g * POOL_GROUP, (g + 1) * POOL_GROUP)

    y = _pool_mix(
        lambda g: h[:, lanes(g)],
        lambda g, j: hext_ref[HIST - j:HIST - j + tile, lanes(g)],
        lambda w: 1.0 / jnp.minimum(pos + 1, w).astype(F32),
        pw_ref)
    x1 = x + (y + pb_ref[...]) * ps_ref[...]
    x2_ref[0] = _ffn(x1, nf_ref[...], wgu_ref, wd_ref)

    hext_ref[0:HIST, :] = hext_ref[tile:tile + HIST, :]

    @pl.when(i == pl.num_programs(1) - 1)
    def _():
        pstate_ref[0, 0] = hext_ref[HIST - POOL_STATE:HIST, :]


def _kvq_prompt_kernel(x2_ref, nkv_ref, nm1_ref, bf_ref, wk_ref, wv_ref, wf_ref, wq_ref, place_ref,
                       k_ref, v_ref, lf_ref, ks_ref, qt_ref, vt_ref, carry_ref, q_ref, *, tile):
    i = pl.program_id(1)

    @pl.when(i == 0)
    def _():
        carry_ref[...] = jnp.zeros(carry_ref.shape, F32)

    x2 = x2_ref[0]
    hk = _rmsnorm(x2, nkv_ref[...]).astype(BF16)
    k = _dot(hk, wk_ref[...])
    v = _dot(hk, wv_ref[...])
    lf = _log_sigmoid(_dot(hk, wf_ref[...]) + bf_ref[...])
    k_ref[0] = k
    v_ref[0] = v
    lf_ref[0] = lf[:, :N_HEADS]

    lane = lax.broadcasted_iota(jnp.int32, (tile, LANES), 1)
    lf = jnp.where(lane < N_HEADS, lf, 0.0)
    fcum = _prefix_sum_rows(lf, _lower_tri(tile)) + carry_ref[0:1, :]
    carry_ref[0:1, :] = fcum[tile - 1:tile, :]

    packed = None
    for n, p in enumerate(_split_bf16(-LOG2E * fcum)):
        p = p.astype(F32)
        p = p if n == 0 else pltpu.roll(p, n * N_HEADS, 1)
        packed = p if packed is None else packed + p
    bias = _dot(packed.astype(BF16), place_ref[...])

    is_data = lane < HEAD_DIM
    for pair in range(N_HEADS // 2):
        cols = k[:, pair * LANES:(pair + 1) * LANES]
        for odd in range(2):
            slab = slice((2 * pair + odd) * LANES, (2 * pair + odd + 1) * LANES)
            data = pltpu.roll(cols, HEAD_DIM, 1) if odd else cols
            ks_ref[0, :, slab] = jnp.where(is_data, data, bias[:, slab]).astype(BF16)

    q_ref[...] = LOG2E * _dot(_rmsnorm(x2, nm1_ref[...]).astype(BF16), wq_ref[...])
    qt = q_ref[...].T
    vt = v_ref[0].T
    fill_row = lax.broadcasted_iota(jnp.int32, (FILL_ROWS, tile), 0)
    q_fill = jnp.where(fill_row < N_SPLIT, 1.0, 0.0).astype(BF16)
    v_fill = jnp.where(fill_row == 0, 1.0, 0.0).astype(BF16)
    q_zero = jnp.zeros((Q_ROWS - HEAD_DIM - FILL_ROWS, tile), BF16)
    for h in range(N_HEADS):
        rows = slice(h * HEAD_DIM, (h + 1) * HEAD_DIM)
        q0, v0 = h * Q_ROWS, h * V_ROWS
        qt_ref[0, q0:q0 + HEAD_DIM, :] = qt[rows, :].astype(BF16)
        qt_ref[0, q0 + HEAD_DIM:q0 + HEAD_DIM + FILL_ROWS, :] = q_fill
        qt_ref[0, q0 + HEAD_DIM + FILL_ROWS:q0 + Q_ROWS, :] = q_zero
        vt_ref[0, 0, v0:v0 + HEAD_DIM, :] = vt[rows, :].astype(BF16)
        vt_ref[0, 0, v0 + HEAD_DIM:v0 + V_ROWS, :] = v_fill


def _attn_prompt_kernel(qt_ref, ks_ref, vt_ref, o_ref, *scratch, tq, tk):
    _attn_prompt_body(pl.program_id(2), qt_ref, ks_ref, vt_ref, o_ref, *scratch, tq=tq, tk=tk)


def _attn_prompt_body(i, qt_ref, ks_ref, vt_ref, o_ref, m_ref, acc_ref, s_ref, p_ref, a_ref, *, tq, tk):
    heads = range(2)
    per_q = tq // tk
    units = [(d, hh) for d in range(per_q) for hh in heads]
    qts = [qt_ref[0, hh * Q_ROWS:(hh + 1) * Q_ROWS, :] for hh in heads]

    def scores(kblk, hh):
        start = pl.multiple_of(kblk * tk, tk)
        return _dot(ks_ref[0, pl.ds(start, tk), hh * LANES:(hh + 1) * LANES], qts[hh])

    def softmax(s, hh):
        m_old = m_ref[hh]
        m_new = jnp.maximum(m_old, jnp.max(s, axis=0, keepdims=True))
        m_ref[hh] = m_new
        return jnp.exp2(s - m_new).astype(BF16), jnp.exp2(m_old - m_new)

    def accumulate(p, alpha, kblk, hh):
        vb = vt_ref[0, kblk, hh * V_ROWS:(hh + 1) * V_ROWS, :]
        acc_ref[hh] = acc_ref[hh] * alpha + _dot(vb, p)

    for hh in heads:
        m_ref[hh] = jnp.full((1, tq), MASKED, F32)
        acc_ref[hh] = jnp.zeros((V_ROWS, tq), F32)

    def value_stage(jj):
        for u, (d, hh) in enumerate(units):
            accumulate(p_ref[u], a_ref[u], jj * per_q + d, hh)

    def softmax_stage(diagonal):
        for u, (d, hh) in enumerate(units):
            s = s_ref[u]
            if diagonal:
                key = lax.broadcasted_iota(jnp.int32, s.shape, 0) + d * tk
                qry = lax.broadcasted_iota(jnp.int32, s.shape, 1)
                s = jnp.where(key <= qry, s, MASKED)
            p_ref[u], a_ref[u] = softmax(s, hh)

    def score_stage(jj):
        for u, (d, hh) in enumerate(units):
            s_ref[u] = scores(jj * per_q + d, hh)

    for u in range(len(units)):
        p_ref[u] = jnp.zeros((tk, tq), BF16)
        a_ref[u] = jnp.ones((1, tq), F32)
    score_stage(0)

    def step(jj):
        value_stage(jnp.maximum(jj - 1, 0))
        softmax_stage(diagonal=False)
        score_stage(jj + 1)

    def two_steps(t, carry):
        step(2 * t)
        step(2 * t + 1)
        return carry

    lax.fori_loop(0, i // 2, two_steps, 0)

    @pl.when(i % 2 == 1)
    def _():
        step(i - 1)

    value_stage(jnp.maximum(i - 1, 0))
    softmax_stage(diagonal=True)
    value_stage(i)

    outs = []
    for hh in heads:
        acc = acc_ref[hh]
        outs.append(acc[0:HEAD_DIM, :] / acc[HEAD_DIM:HEAD_DIM + 1, :])
    o_ref[0] = jnp.concatenate(outs, axis=0).T.astype(BF16)


def _post_kernel(o_ref, x2_ref, wo_ref, nf_ref, wgu_ref, wd_ref, nfin_ref, y_ref):
    x3 = x2_ref[0] + _dot(o_ref[0].astype(BF16), wo_ref[...])
    x4 = _ffn(x3, nf_ref[...], wgu_ref, wd_ref)
    y_ref[0] = _rmsnorm(x4, nfin_ref[...])


def _pre_sample_kernel(xs_ref, st_ref, nm0_ref, nf0_ref, pb_ref, ps_ref, pw_ref, wgu_ref, wd_ref,
                       nkv_ref, nm1_ref, bf_ref, wk_ref, wv_ref, wf_ref, wq_ref,
                       xs2_ref, pst_ref, k_ref, v_ref, lf_ref, q_ref):
    x = xs_ref[...]
    h = _rmsnorm(x, nm0_ref[...])
    for r in range(POOL_STATE - 1):
        pst_ref[r] = st_ref[r + 1]
    pst_ref[POOL_STATE - 1] = h

    def lanes(g):
        return slice(g * POOL_GROUP, (g + 1) * POOL_GROUP)

    y = _pool_mix(
        lambda g: h[:, lanes(g)],
        lambda g, j: st_ref[POOL_STATE - j, :, lanes(g)],
        lambda w: 1.0 / w,
        pw_ref)
    x1 = x + (y + pb_ref[...]) * ps_ref[...]
    x2 = _ffn(x1, nf0_ref[...], wgu_ref, wd_ref)
    xs2_ref[...] = x2
    hk = _rmsnorm(x2, nkv_ref[...]).astype(BF16)
    k_ref[...] = _dot(hk, wk_ref[...])
    v_ref[...] = _dot(hk, wv_ref[...])
    lf_ref[...] = _log_sigmoid(_dot(hk, wf_ref[...]) + bf_ref[...])
    q_ref[...] = _dot(_rmsnorm(x2, nm1_ref[...]).astype(BF16), wq_ref[...])


N_SAMPLE_SCRATCH = 4


def _attn_sample_kernel(pt_ref, qb_ref, kn_ref, vn_ref, lfn_ref, tri_ref, *rest, npg):
    del pt_ref
    pages = (rest[:npg], rest[npg:2 * npg], rest[2 * npg:3 * npg])
    _attn_sample_body(pl.program_id(1), pl.num_programs(1), qb_ref, kn_ref, vn_ref, lfn_ref, tri_ref, *pages,
                      *rest[3 * npg:])


def _attn_fused_kernel(pt_ref, qt_ref, ks_ref, vt_ref, qb_ref, kn_ref, vn_ref, lfn_ref, tri_ref, *rest, npg, tq, tk):
    del pt_ref
    pages = (rest[:npg], rest[npg:2 * npg], rest[2 * npg:3 * npg])
    o_ref, os_ref = rest[3 * npg:3 * npg + 2]
    scratch = rest[3 * npg + 2:]
    i = pl.program_id(2)
    _attn_sample_body(i, pl.num_programs(2), qb_ref, kn_ref, vn_ref, lfn_ref, tri_ref, *pages,
                      os_ref, *scratch[:N_SAMPLE_SCRATCH])
    _attn_prompt_body(i, qt_ref, ks_ref, vt_ref, o_ref, *scratch[N_SAMPLE_SCRATCH:], tq=tq, tk=tk)


def _attn_sample_body(j, n_steps, qb_ref, kn_ref, vn_ref, lfn_ref, tri_ref, k_refs, v_refs, lf_refs,
                      o_ref, m_ref, l_ref, acc_ref, carry_ref):
    npg = len(k_refs)

    @pl.when(j == 0)
    def _():
        m_ref[...] = jnp.full(m_ref.shape, MASKED, F32)
        l_ref[...] = jnp.zeros(l_ref.shape, F32)
        acc_ref[...] = jnp.zeros(acc_ref.shape, F32)
        carry_ref[...] = jnp.zeros(carry_ref.shape, F32)

    def process(pages, valid):
        n = len(pages)
        lf = jnp.concatenate([r[0] for _, _, r in pages], axis=0) if n > 1 else pages[0][2][0]
        inside = None
        for piece in _split_bf16(lf):
            t = _dot(piece, tri_ref[...])
            inside = t if inside is None else inside + t
        total = carry_ref[...]
        logits = []
        for idx, (k_ref, _, _) in enumerate(pages):
            rows = slice(idx * N_HEADS, (idx + 1) * N_HEADS)
            s = jnp.sum(k_ref[0] * qb_ref[0], axis=1)
            u = s - inside[rows] - total
            logits.append(u if valid is None else jnp.where(valid, u, MASKED))
            total = total + inside[rows, PAGE_SIZE - 1:PAGE_SIZE]
        carry_ref[...] = total
        u = jnp.concatenate(logits, axis=1) if n > 1 else logits[0]
        m_old = m_ref[...]
        m_new = jnp.maximum(m_old, jnp.max(u, axis=1, keepdims=True))
        p = jnp.exp(u - m_new)
        alpha = jnp.exp(m_old - m_new)
        l_ref[...] = l_ref[...] * alpha + jnp.sum(p, axis=1, keepdims=True)
        m_ref[...] = m_new
        for h in range(N_HEADS):
            a = acc_ref[h] * alpha[h:h + 1, :]
            for idx, (_, v_ref, _) in enumerate(pages):
                a = a + v_ref[0, h] * p[h:h + 1, idx * PAGE_SIZE:(idx + 1) * PAGE_SIZE]
            acc_ref[h] = a

    process([(k_refs[n], v_refs[n], lf_refs[n]) for n in range(npg)], None)

    @pl.when(j == n_steps - 1)
    def _():
        pos = lax.broadcasted_iota(jnp.int32, (N_HEADS, PAGE_SIZE), 1)
        process([(kn_ref, vn_ref, lfn_ref)], pos == 0)
        o_ref[0] = jnp.sum(acc_ref[...], axis=2) / l_ref[...]


def _const_spec(shape):
    nd = len(shape)
    return pl.BlockSpec(shape, lambda *_: (0,) * nd, pipeline_mode=pl.Buffered(1))


def _params(*sem):
    return pltpu.CompilerParams(dimension_semantics=sem, vmem_limit_bytes=VMEM_LIMIT_BYTES)


def _row(a):
    return a.reshape(1, -1).astype(F32)


def _placement():
    src = jnp.arange(LANES)[:, None]
    dst = jnp.arange(N_HEADS * LANES)[None, :]
    head, off = dst // LANES, dst % LANES - HEAD_DIM
    hit = (off >= 0) & (off < N_SPLIT) & (src == off * N_HEADS + head)
    return hit.astype(BF16)


def _layer0_prompt(x, nm, nf, pb, ps, pw, wgu, wd, tile):
    b, s, d = x.shape
    tok = pl.BlockSpec((1, tile, d), lambda bi, i: (bi, i, 0))
    return pl.pallas_call(
        functools.partial(_layer0_prompt_kernel, tile=tile),
        grid=(b, s // tile),
        in_specs=[tok] + [_const_spec(a.shape) for a in (nm, nf, pb, ps, pw, wgu, wd)],
        out_specs=[tok, pl.BlockSpec((1, 1, POOL_STATE, d), lambda bi, i: (bi, 0, 0, 0))],
        out_shape=[jax.ShapeDtypeStruct((b, s, d), F32),
                   jax.ShapeDtypeStruct((b, 1, POOL_STATE, d), F32)],
        scratch_shapes=[pltpu.VMEM((HIST + tile, d), F32)],
        compiler_params=_params("arbitrary", "arbitrary"),
        name="layer0_prompt",
    )(x, nm, nf, pb, ps, pw, wgu, wd)


def _kvq_prompt(x2, nkv, nm1, bf, wk, wv, wf, wq, tile):
    b, s, d = x2.shape
    place = _placement()
    tok = pl.BlockSpec((1, tile, d), lambda bi, i: (bi, i, 0))
    consts = (nkv, nm1, bf, wk, wv, wf, wq, place)
    return pl.pallas_call(
        functools.partial(_kvq_prompt_kernel, tile=tile),
        grid=(b, s // tile),
        in_specs=[tok] + [_const_spec(a.shape) for a in consts],
        out_specs=[tok, tok, pl.BlockSpec((1, tile, N_HEADS), lambda bi, i: (bi, i, 0)),
                   pl.BlockSpec((1, tile, N_HEADS * LANES), lambda bi, i: (bi, i, 0)),
                   pl.BlockSpec((1, N_HEADS * Q_ROWS, tile), lambda bi, i: (bi, 0, i)),
                   pl.BlockSpec((1, 1, N_HEADS * V_ROWS, tile), lambda bi, i: (bi, i, 0, 0))],
        out_shape=[jax.ShapeDtypeStruct((b, s, d), F32), jax.ShapeDtypeStruct((b, s, d), F32),
                   jax.ShapeDtypeStruct((b, s, N_HEADS), F32),
                   jax.ShapeDtypeStruct((b, s, N_HEADS * LANES), BF16),
                   jax.ShapeDtypeStruct((b, N_HEADS * Q_ROWS, s), BF16),
                   jax.ShapeDtypeStruct((b, s // tile, N_HEADS * V_ROWS, tile), BF16)],
        scratch_shapes=[pltpu.VMEM((SUBLANES, LANES), F32), pltpu.VMEM((tile, d), F32)],
        compiler_params=_params("arbitrary", "arbitrary"),
        name="kvq_prompt",
    )(x2, *consts)


def _prompt_attn_parts(s, tq, tk, ids):
    at = lambda f: (lambda *g: f(*ids(*g)))
    in_specs = [pl.BlockSpec((1, 2 * Q_ROWS, tq), at(lambda bi, hg, i: (bi, hg, i))),
                pl.BlockSpec((1, s, 2 * LANES), at(lambda bi, hg, i: (bi, 0, hg))),
                pl.BlockSpec((1, s // tk, 2 * V_ROWS, tk), at(lambda bi, hg, i: (bi, 0, hg, 0)))]
    out_spec = pl.BlockSpec((1, tq, LANES), at(lambda bi, hg, i: (bi, i, hg)))
    scratch = [pltpu.VMEM((2, 1, tq), F32), pltpu.VMEM((2, V_ROWS, tq), F32),
               pltpu.VMEM((2 * tq // tk, tk, tq), F32), pltpu.VMEM((2 * tq // tk, tk, tq), BF16),
               pltpu.VMEM((2 * tq // tk, 1, tq), F32)]
    return in_specs, out_spec, scratch


def _attn_prompt(qt, ks, vt, tq, tk):
    b, s, _ = ks.shape
    assert vt.shape[1] * tk == s and vt.shape[3] == tk
    in_specs, out_spec, scratch = _prompt_attn_parts(s, tq, tk, lambda bi, hg, i: (bi, hg, i))
    return pl.pallas_call(
        functools.partial(_attn_prompt_kernel, tq=tq, tk=tk),
        grid=(b, N_HEADS // 2, s // tq),
        in_specs=in_specs,
        out_specs=out_spec,
        out_shape=jax.ShapeDtypeStruct((b, s, D_MODEL), BF16),
        scratch_shapes=scratch,
        compiler_params=_params("parallel", "parallel", "arbitrary"),
        name="attn_prompt",
    )(qt, ks, vt)


def _post(o, x2, wo, nf, wgu, wd, nfin, tile):
    b, s, d = x2.shape
    tok = pl.BlockSpec((1, tile, d), lambda bi, i: (bi, i, 0))
    consts = (wo, nf, wgu, wd, nfin)
    return pl.pallas_call(
        _post_kernel,
        grid=(b, s // tile),
        in_specs=[tok, tok] + [_const_spec(a.shape) for a in consts],
        out_specs=tok,
        out_shape=jax.ShapeDtypeStruct((b, s, d), F32),
        compiler_params=_params("parallel", "parallel"),
        name="post",
    )(o, x2, *consts)


def _pre_sample(xs, st, consts):
    n, d = xs.shape
    out_shape = [jax.ShapeDtypeStruct((n, d), F32), jax.ShapeDtypeStruct((POOL_STATE, n, d), F32),
                 jax.ShapeDtypeStruct((n, d), F32), jax.ShapeDtypeStruct((n, d), F32),
                 jax.ShapeDtypeStruct((n, LANES), F32), jax.ShapeDtypeStruct((n, d), F32)]
    return pl.pallas_call(
        _pre_sample_kernel,
        grid=(1,),
        in_specs=[_const_spec(a.shape) for a in (xs, st) + consts],
        out_specs=[pl.BlockSpec(o.shape, lambda i, nd=len(o.shape): (0,) * nd) for o in out_shape],
        out_shape=out_shape,
        compiler_params=_params("arbitrary"),
        name="pre_sample",
    )(xs, st, *consts)


def _sample_attn_parts(q, k_new, v_new, lf_new, cache_k, cache_v, cache_lf, npg, ids):
    cache_k = jnp.transpose(cache_k, (0, 2, 3, 1))
    cache_v = jnp.transpose(cache_v, (0, 2, 3, 1))
    cache_lf = jnp.transpose(cache_lf, (0, 2, 1))
    qb = jnp.broadcast_to(q[..., None], q.shape + (PAGE_SIZE,))
    first = lambda a: jnp.pad(a[..., None], ((0, 0),) * a.ndim + ((0, PAGE_SIZE - 1),))
    pos = jnp.arange(PAGE_SIZE)
    tri = (pos[:, None] <= pos[None, :]).astype(BF16)

    def token(extra):
        return lambda *g: (ids(*g)[0],) + (0,) * extra

    def paged(*minor):
        def spec(p):
            def index(*g):
                t, j, pt = ids(*g)
                return (pt[t, j * npg + p],) + (0,) * len(minor)
            return pl.BlockSpec((1,) + minor, index)
        return [spec(p) for p in range(npg)]

    kv_token = pl.BlockSpec((1, N_HEADS, HEAD_DIM, PAGE_SIZE), token(3))
    in_specs = ([kv_token, kv_token, kv_token, pl.BlockSpec((1, N_HEADS, PAGE_SIZE), token(2)),
                 pl.BlockSpec(tri.shape, lambda *g: (0, 0), pipeline_mode=pl.Buffered(1))]
                + paged(N_HEADS, HEAD_DIM, PAGE_SIZE) + paged(N_HEADS, HEAD_DIM, PAGE_SIZE) + paged(N_HEADS, PAGE_SIZE))
    operands = (qb, first(k_new), first(v_new), first(lf_new), tri,
                *([cache_k] * npg), *([cache_v] * npg), *([cache_lf] * npg))
    out_spec = pl.BlockSpec((1, N_HEADS, HEAD_DIM), token(2))
    scratch = [pltpu.VMEM((N_HEADS, 1), F32), pltpu.VMEM((N_HEADS, 1), F32),
               pltpu.VMEM((N_HEADS, HEAD_DIM, PAGE_SIZE), F32), pltpu.VMEM((N_HEADS, 1), F32)]
    assert len(scratch) == N_SAMPLE_SCRATCH
    return operands, in_specs, out_spec, scratch


def _attn_sample(page_table, sample, npg):
    n, n_pages = page_table.shape
    operands, in_specs, out_spec, scratch = _sample_attn_parts(*sample, npg, lambda t, j, pt: (t, j, pt))
    return pl.pallas_call(
        functools.partial(_attn_sample_kernel, npg=npg),
        grid_spec=pltpu.PrefetchScalarGridSpec(
            num_scalar_prefetch=1, grid=(n, n_pages // npg),
            in_specs=in_specs, out_specs=out_spec, scratch_shapes=scratch),
        out_shape=jax.ShapeDtypeStruct((n, N_HEADS, HEAD_DIM), F32),
        compiler_params=_params("parallel", "arbitrary"),
        name="attn_sample",
    )(page_table, *operands)


def _attn_fused(qt, ks, vt, page_table, sample, tq, tk, npg):
    b, s, _ = ks.shape
    n, n_pages = page_table.shape
    pairs = N_HEADS // 2
    assert n == b * pairs and n_pages // npg == s // tq and vt.shape[1] * tk == s and vt.shape[3] == tk
    p_in, p_out, p_scratch = _prompt_attn_parts(s, tq, tk, lambda bi, hg, i, pt: (bi, hg, i))
    operands, s_in, s_out, s_scratch = _sample_attn_parts(*sample, npg, lambda bi, hg, i, pt: (bi * pairs + hg, i, pt))
    return pl.pallas_call(
        functools.partial(_attn_fused_kernel, npg=npg, tq=tq, tk=tk),
        grid_spec=pltpu.PrefetchScalarGridSpec(
            num_scalar_prefetch=1, grid=(b, pairs, s // tq),
            in_specs=p_in + s_in, out_specs=[p_out, s_out], scratch_shapes=s_scratch + p_scratch),
        out_shape=[jax.ShapeDtypeStruct((b, s, D_MODEL), BF16), jax.ShapeDtypeStruct((n, N_HEADS, HEAD_DIM), F32)],
        compiler_params=_params("parallel", "parallel", "arbitrary"),
        name="attn_fused",
    )(page_table, qt, ks, vt, *operands)


def kernel(x_prompt, x_sample, state_pool, cache_k, cache_v, cache_logf, page_table, norm_mix, norm_ffn,
           pool_w, pool_b, pool_scale, norm_kv, w_kvf, b_f, w_q, w_o, w_gate_up, w_down, norm_final):
    bp, s, d = x_prompt.shape
    bd = x_sample.shape[0]

    pw = pool_w[0].astype(BF16)
    wgu = w_gate_up.astype(BF16)
    wd = w_down.astype(BF16)
    wk = w_kvf[:, :d].astype(BF16)
    wv = w_kvf[:, d:2 * d].astype(BF16)
    wf = jnp.pad(w_kvf[:, 2 * d:], ((0, 0), (0, LANES - N_HEADS))).astype(BF16)
    bf = jnp.pad(b_f, (0, LANES - N_HEADS)).reshape(1, LANES).astype(F32)
    wq = (w_q[0] * ATTN_SCALE).astype(BF16)
    wo = w_o[0].astype(BF16)
    nm0, nm1 = _row(norm_mix[0]), _row(norm_mix[1])
    nf0, nf1 = _row(norm_ffn[0]), _row(norm_ffn[1])
    pb, ps = _row(pool_b[0]), _row(pool_scale[0])
    nkv, nfin = _row(norm_kv), _row(norm_final)

    x2p, pool_state_prompt = _layer0_prompt(x_prompt, nm0, nf0, pb, ps, pw, wgu[0], wd[0], TILE_TOKENS)
    k_p, v_p, lf_p, ks, qt, vt = _kvq_prompt(x2p, nkv, nm1, bf, wk, wv, wf, wq, ATTN_TK)
    xs = x_sample.reshape(bd, d)
    st = jnp.transpose(state_pool[:, 0], (1, 0, 2))
    pre_consts = (nm0, nf0, pb, ps, pw, wgu[0], wd[0], nkv, nm1, bf, wk, wv, wf, wq)
    xs2, pst, k_s, v_s, lf_s, q_s = _pre_sample(xs, st, pre_consts)
    by_head = lambda a: a.reshape(bd, N_HEADS, HEAD_DIM)
    sample = (by_head(q_s), by_head(k_s), by_head(v_s), lf_s[:, :N_HEADS], cache_k, cache_v, cache_logf)

    if bd == bp * (N_HEADS // 2) and page_table.shape[1] // PAGES_PER_STEP == s // ATTN_TQ:
        o_p, o_s = _attn_fused(qt, ks, vt, page_table, sample, ATTN_TQ, ATTN_TK, PAGES_PER_STEP)
    else:
        o_p = _attn_prompt(qt, ks, vt, ATTN_TQ, ATTN_TK)
        o_s = _attn_sample(page_table, sample, PAGES_PER_STEP)

    y_prompt = _post(o_p, x2p, wo, nf1, wgu[1], wd[1], nfin, TILE_TOKENS)
    y_sample = _post(o_s.reshape(1, bd, d), xs2.reshape(1, bd, d), wo, nf1, wgu[1], wd[1], nfin, bd)

    heads = lambda a, n: a.reshape(a.shape[0], n, N_HEADS, HEAD_DIM)
    return (y_prompt, y_sample.reshape(bd, 1, d),
            pool_state_prompt, jnp.transpose(pst, (1, 0, 2)).reshape(bd, 1, POOL_STATE, d),
            heads(k_p, s), heads(v_p, s), lf_p,
            heads(k_s, 1), heads(v_s, 1), lf_s[:, :N_HEADS].reshape(bd, 1, N_HEADS))
```

```python
import functools

import jax
import jax.numpy as jnp
from jax import lax
from jax.experimental import pallas as pl
from jax.experimental.pallas import tpu as pltpu

D_MODEL = 1024
N_HEADS = 16
HEAD_DIM = 64
D_FF = 2816
POOL_WINDOWS = (2, 4, 8, 16)
POOL_GROUP = D_MODEL // len(POOL_WINDOWS)
POOL_STATE = max(POOL_WINDOWS) - 1
EPS = 1e-6
ATTN_SCALE = HEAD_DIM ** -0.5
PAGE_SIZE = 128

LANES = 128
SUBLANES = 8
HIST = 2 * SUBLANES
N_SPLIT = 3
MASKED = -1e30
LOG2E = 1.4426950408889634
Q_ROWS = LANES
FILL_ROWS = 16
V_ROWS = HEAD_DIM + FILL_ROWS
VMEM_LIMIT_BYTES = 56 * 1024 * 1024

TILE_TOKENS = 256
ATTN_TQ = 512
ATTN_TK = 256
PAGES_PER_STEP = 8

F32 = jnp.float32
BF16 = jnp.bfloat16


def _dot(a, b):
    return jnp.dot(a, b, preferred_element_type=F32)


def _rmsnorm(x, g):
    ms = jnp.mean(x * x, axis=-1, keepdims=True)
    return x * lax.rsqrt(ms + EPS) * g


def _log_sigmoid(x):
    return jnp.minimum(x, 0.0) - jnp.log1p(jnp.exp(-jnp.abs(x)))


def _ffn(x, g_norm, wgu_ref, wd_ref):
    h = _rmsnorm(x, g_norm).astype(BF16)
    gu = _dot(h, wgu_ref[...])
    g = gu[:, :D_FF]
    u = gu[:, D_FF:]
    a = g * jax.nn.sigmoid(g) * u
    return x + _dot(a.astype(BF16), wd_ref[...])


def _split_bf16(x):
    pieces = []
    r = x
    for _ in range(N_SPLIT):
        p = r.astype(BF16)
        pieces.append(p)
        r = r - p.astype(F32)
    return pieces


def _lower_tri(n):
    r = lax.broadcasted_iota(jnp.int32, (n, n), 0)
    c = lax.broadcasted_iota(jnp.int32, (n, n), 1)
    return jnp.where(r >= c, 1.0, 0.0).astype(BF16)


def _prefix_sum_rows(x, tri):
    out = None
    for p in _split_bf16(x):
        t = _dot(tri, p)
        out = t if out is None else out + t
    return out


def _pool_mix(cur_fn, shifted_fn, inv_cnt_fn, pw_ref):
    outs = []
    for g, w in enumerate(POOL_WINDOWS):
        cur = cur_fn(g)
        s = cur
        for j in range(1, w):
            s = s + shifted_fn(g, j)
        d = s * inv_cnt_fn(w) - cur
        outs.append(_dot(d.astype(BF16), pw_ref[g]))
    return jnp.concatenate(outs, axis=-1)


def _layer0_prompt_kernel(x_ref, nm_ref, nf_ref, pb_ref, ps_ref, pw_ref, wgu_ref, wd_ref,
                          x2_ref, pstate_ref, hext_ref, *, tile):
    i = pl.program_id(1)

    @pl.when(i == 0)
    def _():
        hext_ref[0:HIST, :] = jnp.zeros((HIST, D_MODEL), F32)

    x = x_ref[0]
    h = _rmsnorm(x, nm_ref[...])
    hext_ref[HIST:HIST + tile, :] = h
    pos = lax.broadcasted_iota(jnp.int32, (tile, 1), 0) + i * tile

    def lanes(g):
        return slice(g * POOL_GROUP, (g + 1) * POOL_GROUP)

    y = _pool_mix(
        lambda g: h[:, lanes(g)],
        lambda g, j: hext_ref[HIST - j:HIST - j + tile, lanes(g)],
        lambda w: 1.0 / jnp.minimum(pos + 1, w).astype(F32),
        pw_ref)
    x1 = x + (y + pb_ref[...]) * ps_ref[...]
    x2_ref[0] = _ffn(x1, nf_ref[...], wgu_ref, wd_ref)

    hext_ref[0:HIST, :] = hext_ref[tile:tile + HIST, :]

    @pl.when(i == pl.num_programs(1) - 1)
    def _():
        pstate_ref[0, 0] = hext_ref[HIST - POOL_STATE:HIST, :]


def _kvq_prompt_kernel(x2_ref, nkv_ref, nm1_ref, bf_ref, wk_ref, wv_ref, wf_ref, wq_ref, place_ref,
                       k_ref, v_ref, lf_ref, ks_ref, qt_ref, vt_ref, carry_ref, q_ref, *, tile):
    i = pl.program_id(1)

    @pl.when(i == 0)
    def _():
        carry_ref[...] = jnp.zeros(carry_ref.shape, F32)

    x2 = x2_ref[0]
    hk = _rmsnorm(x2, nkv_ref[...]).astype(BF16)
    k = _dot(hk, wk_ref[...])
    v = _dot(hk, wv_ref[...])
    lf = _log_sigmoid(_dot(hk, wf_ref[...]) + bf_ref[...])
    k_ref[0] = k
    v_ref[0] = v
    lf_ref[0] = lf[:, :N_HEADS]

    lane = lax.broadcasted_iota(jnp.int32, (tile, LANES), 1)
    lf = jnp.where(lane < N_HEADS, lf, 0.0)
    fcum = _prefix_sum_rows(lf, _lower_tri(tile)) + carry_ref[0:1, :]
    carry_ref[0:1, :] = fcum[tile - 1:tile, :]

    packed = None
    for n, p in enumerate(_split_bf16(-LOG2E * fcum)):
        p = p.astype(F32)
        p = p if n == 0 else pltpu.roll(p, n * N_HEADS, 1)
        packed = p if packed is None else packed + p
    bias = _dot(packed.astype(BF16), place_ref[...])

    is_data = lane < HEAD_DIM
    for pair in range(N_HEADS // 2):
        cols = k[:, pair * LANES:(pair + 1) * LANES]
        for odd in range(2):
            slab = slice((2 * pair + odd) * LANES, (2 * pair + odd + 1) * LANES)
            data = pltpu.roll(cols, HEAD_DIM, 1) if odd else cols
            ks_ref[0, :, slab] = jnp.where(is_data, data, bias[:, slab]).astype(BF16)

    q_ref[...] = LOG2E * _dot(_rmsnorm(x2, nm1_ref[...]).astype(BF16), wq_ref[...])
    qt = q_ref[...].T
    vt = v_ref[0].T
    fill_row = lax.broadcasted_iota(jnp.int32, (FILL_ROWS, tile), 0)
    q_fill = jnp.where(fill_row < N_SPLIT, 1.0, 0.0).astype(BF16)
    v_fill = jnp.where(fill_row == 0, 1.0, 0.0).astype(BF16)
    q_zero = jnp.zeros((Q_ROWS - HEAD_DIM - FILL_ROWS, tile), BF16)
    for h in range(N_HEADS):
        rows = slice(h * HEAD_DIM, (h + 1) * HEAD_DIM)
        q0, v0 = h * Q_ROWS, h * V_ROWS
        qt_ref[0, q0:q0 + HEAD_DIM, :] = qt[rows, :].astype(BF16)
        qt_ref[0, q0 + HEAD_DIM:q0 + HEAD_DIM + FILL_ROWS, :] = q_fill
        qt_ref[0, q0 + HEAD_DIM + FILL_ROWS:q0 + Q_ROWS, :] = q_zero
        vt_ref[0, 0, v0:v0 + HEAD_DIM, :] = vt[rows, :].astype(BF16)
        vt_ref[0, 0, v0 + HEAD_DIM:v0 + V_ROWS, :] = v_fill


def _attn_prompt_kernel(qt_ref, ks_ref, vt_ref, o_ref, *scratch, tq, tk):
    _attn_prompt_body(pl.program_id(2), qt_ref, ks_ref, vt_ref, o_ref, *scratch, tq=tq, tk=tk)


def _attn_prompt_body(i, qt_ref, ks_ref, vt_ref, o_ref, m_ref, acc_ref, s_ref, p_ref, a_ref, *, tq, tk):
    heads = range(2)
    per_q = tq // tk
    units = [(d, hh) for d in range(per_q) for hh in heads]
    qts = [qt_ref[0, hh * Q_ROWS:(hh + 1) * Q_ROWS, :] for hh in heads]

    def scores(kblk, hh):
        start = pl.multiple_of(kblk * tk, tk)
        return _dot(ks_ref[0, pl.ds(start, tk), hh * LANES:(hh + 1) * LANES], qts[hh])

    def softmax(s, hh):
        m_old = m_ref[hh]
        m_new = jnp.maximum(m_old, jnp.max(s, axis=0, keepdims=True))
        m_ref[hh] = m_new
        return jnp.exp2(s - m_new).astype(BF16), jnp.exp2(m_old - m_new)

    def accumulate(p, alpha, kblk, hh):
        vb = vt_ref[0, kblk, hh * V_ROWS:(hh + 1) * V_ROWS, :]
        acc_ref[hh] = acc_ref[hh] * alpha + _dot(vb, p)

    for hh in heads:
        m_ref[hh] = jnp.full((1, tq), MASKED, F32)
        acc_ref[hh] = jnp.zeros((V_ROWS, tq), F32)

    def value_stage(jj):
        for u, (d, hh) in enumerate(units):
            accumulate(p_ref[u], a_ref[u], jj * per_q + d, hh)

    def softmax_stage(diagonal):
        for u, (d, hh) in enumerate(units):
            s = s_ref[u]
            if diagonal:
                key = lax.broadcasted_iota(jnp.int32, s.shape, 0) + d * tk
                qry = lax.broadcasted_iota(jnp.int32, s.shape, 1)
                s = jnp.where(key <= qry, s, MASKED)
            p_ref[u], a_ref[u] = softmax(s, hh)

    def score_stage(jj):
        for u, (d, hh) in enumerate(units):
            s_ref[u] = scores(jj * per_q + d, hh)

    for u in range(len(units)):
        p_ref[u] = jnp.zeros((tk, tq), BF16)
        a_ref[u] = jnp.ones((1, tq), F32)
    score_stage(0)

    def step(jj):
        value_stage(jnp.maximum(jj - 1, 0))
        softmax_stage(diagonal=False)
        score_stage(jj + 1)

    def four_steps(t, carry):
        for r in range(4):
            step(4 * t + r)
        return carry

    lax.fori_loop(0, i // 4, four_steps, 0)
    done = (i // 4) * 4

    @pl.when(i - done >= 2)
    def _():
        step(done)
        step(done + 1)

    @pl.when(i % 2 == 1)
    def _():
        step(i - 1)

    value_stage(jnp.maximum(i - 1, 0))
    softmax_stage(diagonal=True)
    value_stage(i)

    outs = []
    for hh in heads:
        acc = acc_ref[hh]
        outs.append(acc[0:HEAD_DIM, :] / acc[HEAD_DIM:HEAD_DIM + 1, :])
    o_ref[0] = jnp.concatenate(outs, axis=0).T.astype(BF16)


def _post_kernel(o_ref, x2_ref, wo_ref, nf_ref, wgu_ref, wd_ref, nfin_ref, y_ref):
    x3 = x2_ref[0] + _dot(o_ref[0].astype(BF16), wo_ref[...])
    x4 = _ffn(x3, nf_ref[...], wgu_ref, wd_ref)
    y_ref[0] = _rmsnorm(x4, nfin_ref[...])


def _pre_sample_kernel(xs_ref, st_ref, nm0_ref, nf0_ref, pb_ref, ps_ref, pw_ref, wgu_ref, wd_ref,
                       nkv_ref, nm1_ref, bf_ref, wk_ref, wv_ref, wf_ref, wq_ref,
                       xs2_ref, pst_ref, k_ref, v_ref, lf_ref, q_ref):
    x = xs_ref[...]
    h = _rmsnorm(x, nm0_ref[...])
    for r in range(POOL_STATE - 1):
        pst_ref[r] = st_ref[r + 1]
    pst_ref[POOL_STATE - 1] = h

    def lanes(g):
        return slice(g * POOL_GROUP, (g + 1) * POOL_GROUP)

    y = _pool_mix(
        lambda g: h[:, lanes(g)],
        lambda g, j: st_ref[POOL_STATE - j, :, lanes(g)],
        lambda w: 1.0 / w,
        pw_ref)
    x1 = x + (y + pb_ref[...]) * ps_ref[...]
    x2 = _ffn(x1, nf0_ref[...], wgu_ref, wd_ref)
    xs2_ref[...] = x2
    hk = _rmsnorm(x2, nkv_ref[...]).astype(BF16)
    k_ref[...] = _dot(hk, wk_ref[...])
    v_ref[...] = _dot(hk, wv_ref[...])
    lf_ref[...] = _log_sigmoid(_dot(hk, wf_ref[...]) + bf_ref[...])
    q_ref[...] = _dot(_rmsnorm(x2, nm1_ref[...]).astype(BF16), wq_ref[...])


N_SAMPLE_SCRATCH = 4


def _attn_sample_kernel(pt_ref, qb_ref, kn_ref, vn_ref, lfn_ref, tri_ref, *rest, npg):
    del pt_ref
    pages = (rest[:npg], rest[npg:2 * npg], rest[2 * npg:3 * npg])
    _attn_sample_body(pl.program_id(1), pl.num_programs(1), qb_ref, kn_ref, vn_ref, lfn_ref, tri_ref, *pages,
                      *rest[3 * npg:])


def _attn_fused_kernel(pt_ref, qt_ref, ks_ref, vt_ref, qb_ref, kn_ref, vn_ref, lfn_ref, tri_ref, *rest, npg, tq, tk):
    del pt_ref
    pages = (rest[:npg], rest[npg:2 * npg], rest[2 * npg:3 * npg])
    o_ref, os_ref = rest[3 * npg:3 * npg + 2]
    scratch = rest[3 * npg + 2:]
    i = pl.program_id(2)
    _attn_sample_body(i, pl.num_programs(2), qb_ref, kn_ref, vn_ref, lfn_ref, tri_ref, *pages,
                      os_ref, *scratch[:N_SAMPLE_SCRATCH])
    _attn_prompt_body(i, qt_ref, ks_ref, vt_ref, o_ref, *scratch[N_SAMPLE_SCRATCH:], tq=tq, tk=tk)


def _attn_sample_body(j, n_steps, qb_ref, kn_ref, vn_ref, lfn_ref, tri_ref, k_refs, v_refs, lf_refs,
                      o_ref, m_ref, l_ref, acc_ref, carry_ref):
    npg = len(k_refs)

    @pl.when(j == 0)
    def _():
        m_ref[...] = jnp.full(m_ref.shape, MASKED, F32)
        l_ref[...] = jnp.zeros(l_ref.shape, F32)
        acc_ref[...] = jnp.zeros(acc_ref.shape, F32)
        carry_ref[...] = jnp.zeros(carry_ref.shape, F32)

    def process(pages, valid):
        n = len(pages)
        lf = jnp.concatenate([r[0] for _, _, r in pages], axis=0) if n > 1 else pages[0][2][0]
        inside = None
        for piece in _split_bf16(lf):
            t = _dot(piece, tri_ref[...])
            inside = t if inside is None else inside + t
        total = carry_ref[...]
        logits = []
        for idx, (k_ref, _, _) in enumerate(pages):
            rows = slice(idx * N_HEADS, (idx + 1) * N_HEADS)
            s = jnp.sum(k_ref[0] * qb_ref[0], axis=1)
            u = s - inside[rows] - total
            logits.append(u if valid is None else jnp.where(valid, u, MASKED))
            total = total + inside[rows, PAGE_SIZE - 1:PAGE_SIZE]
        carry_ref[...] = total
        u = jnp.concatenate(logits, axis=1) if n > 1 else logits[0]
        m_old = m_ref[...]
        m_new = jnp.maximum(m_old, jnp.max(u, axis=1, keepdims=True))
        p = jnp.exp(u - m_new)
        alpha = jnp.exp(m_old - m_new)
        l_ref[...] = l_ref[...] * alpha + jnp.sum(p, axis=1, keepdims=True)
        m_ref[...] = m_new
        for h in range(N_HEADS):
            a = acc_ref[h] * alpha[h:h + 1, :]
            for idx, (_, v_ref, _) in enumerate(pages):
                a = a + v_ref[0, h] * p[h:h + 1, idx * PAGE_SIZE:(idx + 1) * PAGE_SIZE]
            acc_ref[h] = a

    process([(k_refs[n], v_refs[n], lf_refs[n]) for n in range(npg)], None)

    @pl.when(j == n_steps - 1)
    def _():
        pos = lax.broadcasted_iota(jnp.int32, (N_HEADS, PAGE_SIZE), 1)
        process([(kn_ref, vn_ref, lfn_ref)], pos == 0)
        o_ref[0] = jnp.sum(acc_ref[...], axis=2) / l_ref[...]


def _const_spec(shape):
    nd = len(shape)
    return pl.BlockSpec(shape, lambda *_: (0,) * nd, pipeline_mode=pl.Buffered(1))


def _params(*sem):
    return pltpu.CompilerParams(dimension_semantics=sem, vmem_limit_bytes=VMEM_LIMIT_BYTES)


def _row(a):
    return a.reshape(1, -1).astype(F32)


def _placement():
    src = jnp.arange(LANES)[:, None]
    dst = jnp.arange(N_HEADS * LANES)[None, :]
    head, off = dst // LANES, dst % LANES - HEAD_DIM
    hit = (off >= 0) & (off < N_SPLIT) & (src == off * N_HEADS + head)
    return hit.astype(BF16)


def _layer0_prompt(x, nm, nf, pb, ps, pw, wgu, wd, tile):
    b, s, d = x.shape
    tok = pl.BlockSpec((1, tile, d), lambda bi, i: (bi, i, 0))
    return pl.pallas_call(
        functools.partial(_layer0_prompt_kernel, tile=tile),
        grid=(b, s // tile),
        in_specs=[tok] + [_const_spec(a.shape) for a in (nm, nf, pb, ps, pw, wgu, wd)],
        out_specs=[tok, pl.BlockSpec((1, 1, POOL_STATE, d), lambda bi, i: (bi, 0, 0, 0))],
        out_shape=[jax.ShapeDtypeStruct((b, s, d), F32),
                   jax.ShapeDtypeStruct((b, 1, POOL_STATE, d), F32)],
        scratch_shapes=[pltpu.VMEM((HIST + tile, d), F32)],
        compiler_params=_params("arbitrary", "arbitrary"),
        name="layer0_prompt",
    )(x, nm, nf, pb, ps, pw, wgu, wd)


def _kvq_prompt(x2, nkv, nm1, bf, wk, wv, wf, wq, tile):
    b, s, d = x2.shape
    place = _placement()
    tok = pl.BlockSpec((1, tile, d), lambda bi, i: (bi, i, 0))
    consts = (nkv, nm1, bf, wk, wv, wf, wq, place)
    return pl.pallas_call(
        functools.partial(_kvq_prompt_kernel, tile=tile),
        grid=(b, s // tile),
        in_specs=[tok] + [_const_spec(a.shape) for a in consts],
        out_specs=[tok, tok, pl.BlockSpec((1, tile, N_HEADS), lambda bi, i: (bi, i, 0)),
                   pl.BlockSpec((1, tile, N_HEADS * LANES), lambda bi, i: (bi, i, 0)),
                   pl.BlockSpec((1, N_HEADS * Q_ROWS, tile), lambda bi, i: (bi, 0, i)),
                   pl.BlockSpec((1, 1, N_HEADS * V_ROWS, tile), lambda bi, i: (bi, i, 0, 0))],
        out_shape=[jax.ShapeDtypeStruct((b, s, d), F32), jax.ShapeDtypeStruct((b, s, d), F32),
                   jax.ShapeDtypeStruct((b, s, N_HEADS), F32),
                   jax.ShapeDtypeStruct((b, s, N_HEADS * LANES), BF16),
                   jax.ShapeDtypeStruct((b, N_HEADS * Q_ROWS, s), BF16),
                   jax.ShapeDtypeStruct((b, s // tile, N_HEADS * V_ROWS, tile), BF16)],
        scratch_shapes=[pltpu.VMEM((SUBLANES, LANES), F32), pltpu.VMEM((tile, d), F32)],
        compiler_params=_params("arbitrary", "arbitrary"),
        name="kvq_prompt",
    )(x2, *consts)


def _prompt_attn_parts(s, tq, tk, ids):
    at = lambda f: (lambda *g: f(*ids(*g)))
    in_specs = [pl.BlockSpec((1, 2 * Q_ROWS, tq), at(lambda bi, hg, i: (bi, hg, i))),
                pl.BlockSpec((1, s, 2 * LANES), at(lambda bi, hg, i: (bi, 0, hg))),
                pl.BlockSpec((1, s // tk, 2 * V_ROWS, tk), at(lambda bi, hg, i: (bi, 0, hg, 0)))]
    out_spec = pl.BlockSpec((1, tq, LANES), at(lambda bi, hg, i: (bi, i, hg)))
    scratch = [pltpu.VMEM((2, 1, tq), F32), pltpu.VMEM((2, V_ROWS, tq), F32),
               pltpu.VMEM((2 * tq // tk, tk, tq), F32), pltpu.VMEM((2 * tq // tk, tk, tq), BF16),
               pltpu.VMEM((2 * tq // tk, 1, tq), F32)]
    return in_specs, out_spec, scratch


def _attn_prompt(qt, ks, vt, tq, tk):
    b, s, _ = ks.shape
    assert vt.shape[1] * tk == s and vt.shape[3] == tk
    in_specs, out_spec, scratch = _prompt_attn_parts(s, tq, tk, lambda bi, hg, i: (bi, hg, i))
    return pl.pallas_call(
        functools.partial(_attn_prompt_kernel, tq=tq, tk=tk),
        grid=(b, N_HEADS // 2, s // tq),
        in_specs=in_specs,
        out_specs=out_spec,
        out_shape=jax.ShapeDtypeStruct((b, s, D_MODEL), BF16),
        scratch_shapes=scratch,
        compiler_params=_params("parallel", "parallel", "arbitrary"),
        name="attn_prompt",
    )(qt, ks, vt)


def _post(o, x2, wo, nf, wgu, wd, nfin, tile):
    b, s, d = x2.shape
    tok = pl.BlockSpec((1, tile, d), lambda bi, i: (bi, i, 0))
    consts = (wo, nf, wgu, wd, nfin)
    return pl.pallas_call(
        _post_kernel,
        grid=(b, s // tile),
        in_specs=[tok, tok] + [_const_spec(a.shape) for a in consts],
        out_specs=tok,
        out_shape=jax.ShapeDtypeStruct((b, s, d), F32),
        compiler_params=_params("parallel", "parallel"),
        name="post",
    )(o, x2, *consts)


def _pre_sample(xs, st, consts):
    n, d = xs.shape
    out_shape = [jax.ShapeDtypeStruct((n, d), F32), jax.ShapeDtypeStruct((POOL_STATE, n, d), F32),
                 jax.ShapeDtypeStruct((n, d), F32), jax.ShapeDtypeStruct((n, d), F32),
                 jax.ShapeDtypeStruct((n, LANES), F32), jax.ShapeDtypeStruct((n, d), F32)]
    return pl.pallas_call(
        _pre_sample_kernel,
        grid=(1,),
        in_specs=[_const_spec(a.shape) for a in (xs, st) + consts],
        out_specs=[pl.BlockSpec(o.shape, lambda i, nd=len(o.shape): (0,) * nd) for o in out_shape],
        out_shape=out_shape,
        compiler_params=_params("arbitrary"),
        name="pre_sample",
    )(xs, st, *consts)


def _sample_attn_parts(q_page, k_page, v_page, lf_page, cache_k, cache_v, cache_lf, npg, ids):
    cache_k = jnp.transpose(cache_k, (0, 2, 3, 1))
    cache_v = jnp.transpose(cache_v, (0, 2, 3, 1))
    cache_lf = jnp.transpose(cache_lf, (0, 2, 1))
    pos = jnp.arange(PAGE_SIZE)
    tri = (pos[:, None] <= pos[None, :]).astype(BF16)

    def token(extra):
        return lambda *g: (ids(*g)[0],) + (0,) * extra

    def paged(*minor):
        def spec(p):
            def index(*g):
                t, j, pt = ids(*g)
                return (pt[t, j * npg + p],) + (0,) * len(minor)
            return pl.BlockSpec((1,) + minor, index)
        return [spec(p) for p in range(npg)]

    kv_token = pl.BlockSpec((1, N_HEADS, HEAD_DIM, PAGE_SIZE), token(3))
    in_specs = ([kv_token, kv_token, kv_token, pl.BlockSpec((1, N_HEADS, PAGE_SIZE), token(2)),
                 pl.BlockSpec(tri.shape, lambda *g: (0, 0), pipeline_mode=pl.Buffered(1))]
                + paged(N_HEADS, HEAD_DIM, PAGE_SIZE) + paged(N_HEADS, HEAD_DIM, PAGE_SIZE) + paged(N_HEADS, PAGE_SIZE))
    operands = (q_page, k_page, v_page, lf_page, tri,
                *([cache_k] * npg), *([cache_v] * npg), *([cache_lf] * npg))
    out_spec = pl.BlockSpec((1, N_HEADS, HEAD_DIM), token(2))
    scratch = [pltpu.VMEM((N_HEADS, 1), F32), pltpu.VMEM((N_HEADS, 1), F32),
               pltpu.VMEM((N_HEADS, HEAD_DIM, PAGE_SIZE), F32), pltpu.VMEM((N_HEADS, 1), F32)]
    assert len(scratch) == N_SAMPLE_SCRATCH
    return operands, in_specs, out_spec, scratch


def _attn_sample(page_table, sample, npg):
    n, n_pages = page_table.shape
    operands, in_specs, out_spec, scratch = _sample_attn_parts(*sample, npg, lambda t, j, pt: (t, j, pt))
    return pl.pallas_call(
        functools.partial(_attn_sample_kernel, npg=npg),
        grid_spec=pltpu.PrefetchScalarGridSpec(
            num_scalar_prefetch=1, grid=(n, n_pages // npg),
            in_specs=in_specs, out_specs=out_spec, scratch_shapes=scratch),
        out_shape=jax.ShapeDtypeStruct((n, N_HEADS, HEAD_DIM), F32),
        compiler_params=_params("parallel", "arbitrary"),
        name="attn_sample",
    )(page_table, *operands)


def _attn_fused(qt, ks, vt, page_table, sample, tq, tk, npg):
    b, s, _ = ks.shape
    n, n_pages = page_table.shape
    pairs = N_HEADS // 2
    assert n == b * pairs and n_pages // npg == s // tq and vt.shape[1] * tk == s and vt.shape[3] == tk
    p_in, p_out, p_scratch = _prompt_attn_parts(s, tq, tk, lambda bi, hg, i, pt: (bi, hg, i))
    operands, s_in, s_out, s_scratch = _sample_attn_parts(*sample, npg, lambda bi, hg, i, pt: (bi * pairs + hg, i, pt))
    return pl.pallas_call(
        functools.partial(_attn_fused_kernel, npg=npg, tq=tq, tk=tk),
        grid_spec=pltpu.PrefetchScalarGridSpec(
            num_scalar_prefetch=1, grid=(b, pairs, s // tq),
            in_specs=p_in + s_in, out_specs=[p_out, s_out], scratch_shapes=s_scratch + p_scratch),
        out_shape=[jax.ShapeDtypeStruct((b, s, D_MODEL), BF16), jax.ShapeDtypeStruct((n, N_HEADS, HEAD_DIM), F32)],
        compiler_params=_params("parallel", "parallel", "arbitrary"),
        name="attn_fused",
    )(page_table, qt, ks, vt, *operands)


def kernel(x_prompt, x_sample, state_pool, cache_k, cache_v, cache_logf, page_table, norm_mix, norm_ffn,
           pool_w, pool_b, pool_scale, norm_kv, w_kvf, b_f, w_q, w_o, w_gate_up, w_down, norm_final):
    bp, s, d = x_prompt.shape
    bd = x_sample.shape[0]

    pw = pool_w[0].astype(BF16)
    wgu = [w.astype(BF16) for w in w_gate_up]
    wd = [w.astype(BF16) for w in w_down]
    wk = w_kvf[:, :d].astype(BF16)
    wv = w_kvf[:, d:2 * d].astype(BF16)
    wf = jnp.pad(w_kvf[:, 2 * d:], ((0, 0), (0, LANES - N_HEADS))).astype(BF16)
    bf = jnp.pad(b_f, (0, LANES - N_HEADS)).reshape(1, LANES).astype(F32)
    wq = (w_q[0] * ATTN_SCALE).astype(BF16)
    wo = w_o[0].astype(BF16)
    nm0, nm1 = _row(norm_mix[0]), _row(norm_mix[1])
    nf0, nf1 = _row(norm_ffn[0]), _row(norm_ffn[1])
    pb, ps = _row(pool_b[0]), _row(pool_scale[0])
    nkv, nfin = _row(norm_kv), _row(norm_final)

    x2p, pool_state_prompt = _layer0_prompt(x_prompt, nm0, nf0, pb, ps, pw, wgu[0], wd[0], TILE_TOKENS)
    k_p, v_p, lf_p, ks, qt, vt = _kvq_prompt(x2p, nkv, nm1, bf, wk, wv, wf, wq, ATTN_TK)
    xs = x_sample.reshape(bd, d)
    st = jnp.transpose(state_pool[:, 0], (1, 0, 2))
    pre_consts = (nm0, nf0, pb, ps, pw, wgu[0], wd[0], nkv, nm1, bf, wk, wv, wf, wq)
    xs2, pst, k_s, v_s, lf_s, q_s = _pre_sample(xs, st, pre_consts)
    by_head = lambda a: a.reshape(bd, N_HEADS, HEAD_DIM)
    first = lambda a: jnp.pad(a[..., None], ((0, 0),) * a.ndim + ((0, PAGE_SIZE - 1),))
    q_page = jnp.broadcast_to(by_head(q_s)[..., None], (bd, N_HEADS, HEAD_DIM, PAGE_SIZE))
    sample = (q_page, first(by_head(k_s)), first(by_head(v_s)), first(lf_s[:, :N_HEADS]), cache_k, cache_v, cache_logf)

    if bd == bp * (N_HEADS // 2) and page_table.shape[1] // PAGES_PER_STEP == s // ATTN_TQ:
        o_p, o_s = _attn_fused(qt, ks, vt, page_table, sample, ATTN_TQ, ATTN_TK, PAGES_PER_STEP)
    else:
        o_p = _attn_prompt(qt, ks, vt, ATTN_TQ, ATTN_TK)
        o_s = _attn_sample(page_table, sample, PAGES_PER_STEP)

    y_prompt = _post(o_p, x2p, wo, nf1, wgu[1], wd[1], nfin, TILE_TOKENS)
    y_sample = _post(o_s.reshape(1, bd, d), xs2.reshape(1, bd, d), wo, nf1, wgu[1], wd[1], nfin, bd)

    heads = lambda a, n: a.reshape(a.shape[0], n, N_HEADS, HEAD_DIM)
    return (y_prompt, y_sample.reshape(bd, 1, d),
            pool_state_prompt, jnp.transpose(pst, (1, 0, 2)).reshape(bd, 1, POOL_STATE, d),
            heads(k_p, s), heads(v_p, s), lf_p,
            heads(k_s, 1), heads(v_s, 1), lf_s[:, :N_HEADS].reshape(bd, 1, N_HEADS))
```
